```python
import math
import jax, jax.numpy as jnp
from jax import lax
import numpy as np

D_MODEL = 1024
BATCH = 8
SEQ = 8192
DEPTH = 1
DEC_BATCH = 8
DEC_SEQ = 4096
PAST_LEN = 128

GRID_W = 64
HEAD_DIM = 64
DA_VDIM = 2 * HEAD_DIM
DA_WIDTH = D_MODEL // 2
DA_HEADS = DA_WIDTH // DA_VDIM
NA_WIDTH = D_MODEL - DA_WIDTH
NA_HEADS = NA_WIDTH // HEAD_DIM
NA_WIN_R = 8
NA_WIN_C = 16
IN_COLS = 3 * DA_WIDTH + 3 * NA_WIDTH
Q_BLOCK = 128
PEER_HEADS = 8
PEER_KEYS = 128
PEER_EXPERTS = PEER_KEYS * PEER_KEYS
PEER_QDIM = 256
PEER_HALF = PEER_QDIM // 2
PEER_TOPK = 16
PEER_CHUNK = 128
RMS_EPS = 1e-6

kernel_name = "hymba_diffattn_natten_peer_encoder"


def rmsnorm(x, g):
    xf = x.astype(jnp.float32)
    y = xf * lax.rsqrt(jnp.mean(xf * xf, axis=-1, keepdims=True) + RMS_EPS)
    return (y * g.astype(jnp.float32)).astype(x.dtype)


def alibi_slopes(n):
    return np.array([2.0 ** (-8.0 * (h + 1) / n) for h in range(n)], dtype=np.float32)


def diff_attention(q, k, v, lam, subln_g, lambda_init):
    B, S = q.shape[0], q.shape[1]
    nblk = S // Q_BLOCK
    qb = jnp.moveaxis(q.reshape(B, nblk, Q_BLOCK, DA_HEADS, 2, HEAD_DIM), 1, 0)
    slopes = jnp.asarray(alibi_slopes(DA_HEADS))
    kpos = jnp.arange(S, dtype=jnp.float32)
    scale = HEAD_DIM ** -0.5

    def block(args):
        qblk, i = args
        qpos = (i * Q_BLOCK + jnp.arange(Q_BLOCK)).astype(jnp.float32)
        s = jnp.einsum('bqhcd,bkhcd->bhcqk', qblk, k).astype(jnp.float32) * scale
        dist = jnp.abs(qpos[:, None] - kpos[None, :])
        s = s - (slopes[:, None, None, None] * dist)[None]
        p = jax.nn.softmax(s, axis=-1)
        a = p[:, :, 0] - lam * p[:, :, 1]
        return jnp.einsum('bhqk,bkhe->bqhe', a.astype(v.dtype), v)

    o = lax.map(block, (qb, jnp.arange(nblk)))
    o = jnp.moveaxis(o, 0, 1).reshape(B, S, DA_HEADS, DA_VDIM)
    o = rmsnorm(o, subln_g) * (1.0 - lambda_init)
    return o.reshape(B, S, DA_WIDTH)


def neighborhood_attention(q, k, v, rpb):
    B, S = q.shape[0], q.shape[1]
    rows = S // GRID_W
    kr = min(NA_WIN_R, rows)
    kg = k.reshape(B, rows, GRID_W, NA_HEADS, HEAD_DIM)
    vg = v.reshape(B, rows, GRID_W, NA_HEADS, HEAD_DIM)
    qr = jnp.moveaxis(q.reshape(B, rows, GRID_W, NA_HEADS, HEAD_DIM), 1, 0)
    cols = np.arange(GRID_W)
    cstart = np.clip(cols - NA_WIN_C // 2, 0, GRID_W - NA_WIN_C)
    col_idx = cstart[:, None] + np.arange(NA_WIN_C)[None, :]
    dc = col_idx - cols[:, None] + (NA_WIN_C - 1)
    rpb_c = rpb[:, :, dc]
    scale = HEAD_DIM ** -0.5

    def row(args):
        q_row, r = args
        rs = jnp.clip(r - kr // 2, 0, rows - kr)
        k_blk = lax.dynamic_slice_in_dim(kg, rs, kr, axis=1)
        v_blk = lax.dynamic_slice_in_dim(vg, rs, kr, axis=1)
        k_win = k_blk[:, :, col_idx]
        v_win = v_blk[:, :, col_idx]
        dr = rs + jnp.arange(kr) - r + (NA_WIN_R - 1)
        bias = jnp.transpose(rpb_c[:, dr], (0, 2, 1, 3))
        s = jnp.einsum('bqhd,brqchd->bhqrc', q_row, k_win).astype(jnp.float32) * scale
        s = s + bias[None].astype(jnp.float32)
        p = jax.nn.softmax(s.reshape(B, NA_HEADS, GRID_W, kr * NA_WIN_C), axis=-1)
        p = p.reshape(B, NA_HEADS, GRID_W, kr, NA_WIN_C)
        return jnp.einsum('bhqrc,brqchd->bqhd', p.astype(v.dtype), v_win)

    o = lax.map(row, (qr, jnp.arange(rows)))
    return jnp.moveaxis(o, 0, 1).reshape(B, S, NA_WIDTH)


def peer(x, w_query, sub_keys, u, v):
    B, S, D = x.shape
    n = (B * S) // PEER_CHUNK
    xc = x.reshape(n, PEER_CHUNK, D)

    def chunk(xb):
        q = (xb @ w_query).reshape(PEER_CHUNK, PEER_HEADS, 2, PEER_HALF)
        s = jnp.einsum('chpd,hpnd->chpn', q, sub_keys).astype(jnp.float32)
        s1, i1 = lax.top_k(s[:, :, 0], PEER_TOPK)
        s2, i2 = lax.top_k(s[:, :, 1], PEER_TOPK)
        cand = (s1[..., :, None] + s2[..., None, :]).reshape(PEER_CHUNK, PEER_HEADS, PEER_TOPK * PEER_TOPK)
        cidx = (i1[..., :, None] * PEER_KEYS + i2[..., None, :]).reshape(PEER_CHUNK, PEER_HEADS, PEER_TOPK * PEER_TOPK)
        top, pos = lax.top_k(cand, PEER_TOPK)
        idx = jnp.take_along_axis(cidx, pos, axis=-1)
        g = jax.nn.softmax(top, axis=-1)
        ue = u[idx]
        h = jnp.einsum('chkd,cd->chk', ue, xb).astype(jnp.float32)
        w = (g * jax.nn.gelu(h)).astype(x.dtype)
        ve = v[idx]
        return jnp.einsum('chk,chkd->cd', w, ve)

    return lax.map(chunk, xc).reshape(B, S, D)


def encoder_layer(x, layer_idx, ln1_g, w_in, da_q_norm_g, da_k_norm_g, lam_q1, lam_k1, lam_q2, lam_k2,
                  da_subln_g, na_q_norm_g, na_k_norm_g, na_rpb, w_out, ln2_g,
                  peer_w_query, peer_sub_keys, peer_u, peer_v):
    B, S, _ = x.shape
    h = rmsnorm(x, ln1_g)
    proj = h @ w_in
    splits = [DA_WIDTH, 2 * DA_WIDTH, 3 * DA_WIDTH,
              3 * DA_WIDTH + NA_WIDTH, 3 * DA_WIDTH + 2 * NA_WIDTH]
    dq, dk, dv, nq, nk, nv = jnp.split(proj, splits, axis=-1)
    dq = rmsnorm(dq.reshape(B, S, DA_HEADS, 2, HEAD_DIM), da_q_norm_g)
    dk = rmsnorm(dk.reshape(B, S, DA_HEADS, 2, HEAD_DIM), da_k_norm_g)
    dv = dv.reshape(B, S, DA_HEADS, DA_VDIM)
    lambda_init = 0.8 - 0.6 * math.exp(-0.3 * layer_idx)
    lam = (jnp.exp(jnp.sum(lam_q1.astype(jnp.float32) * lam_k1.astype(jnp.float32)))
           - jnp.exp(jnp.sum(lam_q2.astype(jnp.float32) * lam_k2.astype(jnp.float32)))
           + lambda_init)
    o_da = diff_attention(dq, dk, dv, lam, da_subln_g, lambda_init)
    nq = rmsnorm(nq.reshape(B, S, NA_HEADS, HEAD_DIM), na_q_norm_g)
    nk = rmsnorm(nk.reshape(B, S, NA_HEADS, HEAD_DIM), na_k_norm_g)
    nv = nv.reshape(B, S, NA_HEADS, HEAD_DIM)
    o_na = neighborhood_attention(nq, nk, nv, na_rpb)
    x = x + jnp.concatenate([o_da, o_na], axis=-1) @ w_out
    x = x + peer(rmsnorm(x, ln2_g), peer_w_query, peer_sub_keys, peer_u, peer_v)
    return x


def setup_inputs(seed: int = 0) -> dict:
    key = jax.random.key(seed)
    ks = jax.random.split(key, 24)
    f32 = jnp.float32
    nrm = lambda k, shape, s: jax.random.normal(k, shape, f32) * s
    return {
        "x_prompt": nrm(ks[0], (BATCH, SEQ, D_MODEL), 1.0),
        "x_sample": nrm(ks[1], (DEC_BATCH, DEC_SEQ, D_MODEL), 1.0),
        "ln1_g": 1.0 + nrm(ks[2], (DEPTH, D_MODEL), 0.02),
        "w_in": nrm(ks[3], (DEPTH, D_MODEL, IN_COLS), D_MODEL ** -0.5),
        "da_q_norm_g": 1.0 + nrm(ks[4], (DEPTH, HEAD_DIM), 0.02),
        "da_k_norm_g": 1.0 + nrm(ks[5], (DEPTH, HEAD_DIM), 0.02),
        "da_lambda_q1": nrm(ks[6], (DEPTH, HEAD_DIM), 0.1),
        "da_lambda_k1": nrm(ks[7], (DEPTH, HEAD_DIM), 0.1),
        "da_lambda_q2": nrm(ks[8], (DEPTH, HEAD_DIM), 0.1),
        "da_lambda_k2": nrm(ks[9], (DEPTH, HEAD_DIM), 0.1),
        "da_subln_g": 1.0 + nrm(ks[10], (DEPTH, DA_VDIM), 0.02),
        "na_q_norm_g": 1.0 + nrm(ks[11], (DEPTH, HEAD_DIM), 0.02),
        "na_k_norm_g": 1.0 + nrm(ks[12], (DEPTH, HEAD_DIM), 0.02),
        "na_rpb": nrm(ks[13], (DEPTH, NA_HEADS, 2 * NA_WIN_R - 1, 2 * NA_WIN_C - 1), 0.5),
        "w_out": nrm(ks[14], (DEPTH, DA_WIDTH + NA_WIDTH, D_MODEL), (DA_WIDTH + NA_WIDTH) ** -0.5),
        "ln2_g": 1.0 + nrm(ks[15], (DEPTH, D_MODEL), 0.02),
        "peer_w_query": nrm(ks[16], (DEPTH, D_MODEL, PEER_HEADS * PEER_QDIM), D_MODEL ** -0.5),
        "peer_sub_keys": nrm(ks[17], (DEPTH, PEER_HEADS, 2, PEER_KEYS, PEER_HALF), PEER_HALF ** -0.5),
        "peer_u": nrm(ks[18], (DEPTH, PEER_EXPERTS, D_MODEL), D_MODEL ** -0.5),
        "peer_v": nrm(ks[19], (DEPTH, PEER_EXPERTS, D_MODEL), 0.3),
    }


def reference(x_prompt, x_sample, ln1_g, w_in, da_q_norm_g, da_k_norm_g, da_lambda_q1, da_lambda_k1,
              da_lambda_q2, da_lambda_k2, da_subln_g, na_q_norm_g, na_k_norm_g, na_rpb, w_out, ln2_g,
              peer_w_query, peer_sub_keys, peer_u, peer_v):
    hp = x_prompt
    hs = x_sample
    for l in range(DEPTH):
        params = (ln1_g[l], w_in[l], da_q_norm_g[l], da_k_norm_g[l], da_lambda_q1[l], da_lambda_k1[l],
                  da_lambda_q2[l], da_lambda_k2[l], da_subln_g[l], na_q_norm_g[l], na_k_norm_g[l],
                  na_rpb[l], w_out[l], ln2_g[l], peer_w_query[l], peer_sub_keys[l], peer_u[l], peer_v[l])
        hp = encoder_layer(hp, l, *params)
        hs = encoder_layer(hs, l, *params)
    return (hp, hs)
```

```python
import functools
import math

import numpy as np
import jax
import jax.numpy as jnp
from jax import lax
from jax.experimental import pallas as pl
from jax.experimental.pallas import tpu as pltpu

F32 = jnp.float32
BF16 = jnp.bfloat16

D_MODEL = 1024
HEAD_DIM = 64
LANES = 128
DA_HEADS = 4
DA_WIDTH = 512
NA_HEADS = 8
NA_WIDTH = 512
GRID_W = 64
NA_WIN_R = 8
NA_WIN_C = 16
IN_COLS = 3 * DA_WIDTH + 3 * NA_WIDTH
PEER_HEADS = 8
PEER_KEYS = 128
PEER_QDIM = 256
PEER_TOPK = 16
PEER_EXPERTS = PEER_KEYS * PEER_KEYS
RMS_EPS = 1e-6
NEG = -1e30
VMEM_LIMIT = 56 * 1024 * 1024

_DQ, _DK, _DV = 0, 4, 8
_NQ, _NK, _NV = 12, 16, 20


def _cparams(sem):
    return pltpu.CompilerParams(dimension_semantics=sem, vmem_limit_bytes=VMEM_LIMIT)


def _group_ms(x, gm):
    sq = x * x
    hi = sq.astype(BF16)
    lo = (sq - hi.astype(F32)).astype(BF16)
    return (jnp.dot(hi, gm, preferred_element_type=F32)
            + jnp.dot(lo, gm, preferred_element_type=F32))


def _inproj_kernel(x_ref, g1_ref, w_ref, gn_ref, gm_ref, o_ref):
    x = x_ref[...]
    ms = jnp.mean(x * x, axis=-1, keepdims=True)
    xn = (x * lax.rsqrt(ms + RMS_EPS) * g1_ref[...]).astype(BF16)
    proj = jnp.dot(xn, w_ref[...], preferred_element_type=F32)
    gm = gm_ref[...]
    for sec, row in ((0, 0), (1, 1), (3, 2), (4, 3)):
        xs = proj[:, sec * 512:(sec + 1) * 512]
        y = xs * lax.rsqrt(_group_ms(xs, gm) + RMS_EPS) * gn_ref[row:row + 1, :]
        o_ref[:, sec * 512:(sec + 1) * 512] = y.astype(BF16)
    for sec in (2, 5):
        o_ref[:, sec * 512:(sec + 1) * 512] = proj[:, sec * 512:(sec + 1) * 512].astype(BF16)


def _inproj(x2d, g1, w_in_bf, gn, gm, tm=256):
    T = x2d.shape[0]
    return pl.pallas_call(
        _inproj_kernel,
        grid=(T // tm,),
        in_specs=[
            pl.BlockSpec((tm, D_MODEL), lambda i: (i, 0)),
            pl.BlockSpec((1, D_MODEL), lambda i: (0, 0)),
            pl.BlockSpec((D_MODEL, IN_COLS), lambda i: (0, 0)),
            pl.BlockSpec((4, 512), lambda i: (0, 0)),
            pl.BlockSpec((512, 512), lambda i: (0, 0)),
        ],
        out_specs=pl.BlockSpec((tm, IN_COLS), lambda i: (i, 0)),
        out_shape=jax.ShapeDtypeStruct((T, IN_COLS), BF16),
        compiler_params=_cparams(("parallel",)),
        name="inproj",
    )(x2d, g1, w_in_bf, gn, gm)


def _da_kernel(sc_ref, q_ref, k_ref, v_ref, g_ref, o_ref, qs_ref, m_ref, l_ref, acc_ref, *, tq, tk, nk):
    h = pl.program_id(1)
    qi = pl.program_id(2)
    slope = sc_ref[h]
    lam = sc_ref[4]
    post = sc_ref[5]

    q = q_ref[...]
    lane = lax.broadcasted_iota(jnp.int32, (tq, LANES), 1)
    zero = jnp.zeros_like(q)
    qs_ref[0:tq, :] = jnp.where(lane < HEAD_DIM, q, zero)
    qs_ref[tq:2 * tq, :] = jnp.where(lane >= HEAD_DIM, q, zero)
    m_ref[...] = jnp.full((2 * tq, 1), NEG, F32)
    l_ref[...] = jnp.zeros((2 * tq, 1), F32)
    acc_ref[...] = jnp.zeros((2 * tq, LANES), F32)

    row = lax.broadcasted_iota(jnp.int32, (2 * tq, tk), 0)
    col = lax.broadcasted_iota(jnp.int32, (2 * tq, tk), 1)
    rel = (jnp.where(row >= tq, row - tq, row) - col).astype(F32)

    def body(ki, carry):
        k0 = pl.multiple_of(ki * tk, tk)
        kb = k_ref[pl.ds(k0, tk), :]
        vb = v_ref[pl.ds(k0, tk), :]
        s = lax.dot_general(qs_ref[...], kb, (((1,), (1,)), ((), ())), preferred_element_type=F32)
        base = (qi * tq - ki * tk).astype(F32)
        s = s - slope * jnp.abs(rel + base)
        m_old = m_ref[...]
        m_new = jnp.maximum(m_old, jnp.max(s, axis=-1, keepdims=True))
        alpha = jnp.exp(m_old - m_new)
        p = jnp.exp(s - m_new)
        l_ref[...] = alpha * l_ref[...] + jnp.sum(p, axis=-1, keepdims=True)
        acc_ref[...] = alpha * acc_ref[...] + jnp.dot(p.astype(BF16), vb, preferred_element_type=F32)
        m_ref[...] = m_new
        return carry

    lax.fori_loop(0, nk, body, 0)

    on = acc_ref[...] / l_ref[...]
    o = on[0:tq, :] - lam * on[tq:2 * tq, :]
    ms = jnp.mean(o * o, axis=-1, keepdims=True)
    o = o * lax.rsqrt(ms + RMS_EPS) * g_ref[...] * post
    o_ref[...] = o.astype(BF16)


def _diff_attention(qkv, scal, subln_g, B, S, tq=512, tk=512):
    T = B * S
    nq, nk = S // tq, S // tk
    kern = functools.partial(_da_kernel, tq=tq, tk=tk, nk=nk)
    return pl.pallas_call(
        kern,
        grid=(B, DA_HEADS, nq),
        in_specs=[
            pl.BlockSpec(memory_space=pltpu.SMEM),
            pl.BlockSpec((tq, LANES), lambda b, h, i: (b * nq + i, _DQ + h)),
            pl.BlockSpec((S, LANES), lambda b, h, i: (b, _DK + h)),
            pl.BlockSpec((S, LANES), lambda b, h, i: (b, _DV + h)),
            pl.BlockSpec((1, LANES), lambda b, h, i: (0, 0)),
        ],
        out_specs=pl.BlockSpec((tq, LANES), lambda b, h, i: (b * nq + i, h)),
        out_shape=jax.ShapeDtypeStruct((T, DA_WIDTH), BF16),
        scratch_shapes=[
            pltpu.VMEM((2 * tq, LANES), BF16),
            pltpu.VMEM((2 * tq, 1), F32),
            pltpu.VMEM((2 * tq, 1), F32),
            pltpu.VMEM((2 * tq, LANES), F32),
        ],
        compiler_params=_cparams(("parallel", "parallel", "parallel")),
        name="diff_attn",
    )(scal, qkv, qkv, qkv, subln_g)


NA_QROWS = 8
NA_KROWS = 16
NA_TQ = NA_QROWS * GRID_W
NA_TK = NA_KROWS * GRID_W


def _na_bias_tables(rpb, rows):
    cols = np.arange(GRID_W)
    cstart = np.clip(cols - NA_WIN_C // 2, 0, GRID_W - NA_WIN_C)
    kc = cols[None, :]
    col_ok = (kc >= cstart[:, None]) & (kc < cstart[:, None] + NA_WIN_C)
    dc = np.clip(kc - cols[:, None] + (NA_WIN_C - 1), 0, 2 * NA_WIN_C - 2)
    t = jnp.where(jnp.asarray(col_ok)[None, None], rpb[:, :, dc], NEG)
    t = jnp.concatenate([t, jnp.full((NA_HEADS, 1, GRID_W, GRID_W), NEG, F32)], axis=1)
    tables = []
    for r0, w0 in ((0, 0), (rows // 2 // NA_QROWS * NA_QROWS, None), (rows - NA_QROWS, rows - NA_KROWS)):
        if w0 is None:
            r0 = max(NA_QROWS, min(r0, rows - 2 * NA_QROWS))
            w0 = r0 - NA_WIN_R // 2
        r = r0 + np.arange(NA_QROWS)[:, None]
        kr = w0 + np.arange(NA_KROWS)[None, :]
        rs = np.clip(r - NA_WIN_R // 2, 0, rows - NA_WIN_R)
        ok = (kr >= rs) & (kr < rs + NA_WIN_R)
        dr = np.where(ok, kr - r + (NA_WIN_R - 1), 2 * NA_WIN_R - 1)
        tb = t[:, dr]
        tables.append(jnp.transpose(tb, (0, 1, 3, 2, 4)).reshape(NA_HEADS, NA_TQ, NA_TK))
    return jnp.stack(tables)


def _na_kernel(q_ref, k_ref, v_ref, b_ref, o_ref, *, rows):
    j = pl.program_id(2)
    w0 = jnp.clip(j * NA_QROWS - NA_WIN_R // 2, 0, rows - NA_KROWS)
    start = pl.multiple_of(w0 * GRID_W, GRID_W)
    kw = k_ref[pl.ds(start, NA_TK), :]
    vw = v_ref[pl.ds(start, NA_TK), :]
    q = q_ref[...]
    lane = lax.broadcasted_iota(jnp.int32, (NA_TQ, LANES), 1)
    zero = jnp.zeros_like(q)
    outs = []
    for hh in range(2):
        msk = (lane < HEAD_DIM) if hh == 0 else (lane >= HEAD_DIM)
        qm = jnp.where(msk, q, zero)
        s = lax.dot_general(qm, kw, (((1,), (1,)), ((), ())), preferred_element_type=F32)
        s = s + b_ref[0, hh]
        m = jnp.max(s, axis=-1, keepdims=True)
        p = jnp.exp(s - m)
        l = jnp.sum(p, axis=-1, keepdims=True)
        outs.append(jnp.dot(p.astype(BF16), vw, preferred_element_type=F32) / l)
    o_ref[...] = jnp.where(lane < HEAD_DIM, outs[0], outs[1]).astype(BF16)


def _neighborhood_attention(qkv, tables, B, S):
    T = B * S
    rows = S // GRID_W
    nblk = rows // NA_QROWS
    kern = functools.partial(_na_kernel, rows=rows)

    def btype(j):
        return jnp.where(j == 0, 0, jnp.where(j == nblk - 1, 2, 1))

    return pl.pallas_call(
        kern,
        grid=(NA_HEADS // 2, B, nblk),
        in_specs=[
            pl.BlockSpec((NA_TQ, LANES), lambda p, b, j: (b * nblk + j, _NQ + p)),
            pl.BlockSpec((S, LANES), lambda p, b, j: (b, _NK + p)),
            pl.BlockSpec((S, LANES), lambda p, b, j: (b, _NV + p)),
            pl.BlockSpec((1, 2, NA_TQ, NA_TK), lambda p, b, j: (btype(j), p, 0, 0)),
        ],
        out_specs=pl.BlockSpec((NA_TQ, LANES), lambda p, b, j: (b * nblk + j, p)),
        out_shape=jax.ShapeDtypeStruct((T, NA_WIDTH), BF16),
        compiler_params=_cparams(("parallel", "parallel", "parallel")),
        name="nbr_attn",
    )(qkv, qkv, qkv, tables)


def _mix_kernel(x_ref, oda_ref, ona_ref, wo_ref, g2_ref, wqt_ref, sk_ref, x1_ref, xn_ref, st_ref):
    x1 = (x_ref[...]
          + jnp.dot(oda_ref[...], wo_ref[0:DA_WIDTH, :], preferred_element_type=F32)
          + jnp.dot(ona_ref[...], wo_ref[DA_WIDTH:, :], preferred_element_type=F32))
    x1_ref[...] = x1
    ms = jnp.mean(x1 * x1, axis=-1, keepdims=True)
    xn = (x1 * lax.rsqrt(ms + RMS_EPS) * g2_ref[...]).astype(BF16)
    xn_ref[...] = xn
    qt = lax.dot_general(wqt_ref[...], xn, (((1,), (1,)), ((), ())), preferred_element_type=F32).astype(BF16)
    for hp in range(2 * PEER_HEADS):
        st_ref[hp * PEER_KEYS:(hp + 1) * PEER_KEYS, :] = jnp.dot(
            sk_ref[hp], qt[hp * LANES:(hp + 1) * LANES, :], preferred_element_type=F32)


def _mix(x2d, oda, ona, wo_bf, g2, wqt_bf, sk_bf, tm=256):
    T = x2d.shape[0]
    nsc = 2 * PEER_HEADS * PEER_KEYS
    return pl.pallas_call(
        _mix_kernel,
        grid=(T // tm,),
        in_specs=[
            pl.BlockSpec((tm, D_MODEL), lambda i: (i, 0)),
            pl.BlockSpec((tm, DA_WIDTH), lambda i: (i, 0)),
            pl.BlockSpec((tm, NA_WIDTH), lambda i: (i, 0)),
            pl.BlockSpec((D_MODEL, D_MODEL), lambda i: (0, 0)),
            pl.BlockSpec((1, D_MODEL), lambda i: (0, 0)),
            pl.BlockSpec((nsc, D_MODEL), lambda i: (0, 0)),
            pl.BlockSpec((2 * PEER_HEADS, PEER_KEYS, LANES), lambda i: (0, 0, 0)),
        ],
        out_specs=[
            pl.BlockSpec((tm, D_MODEL), lambda i: (i, 0)),
            pl.BlockSpec((tm, D_MODEL), lambda i: (i, 0)),
            pl.BlockSpec((nsc, tm), lambda i: (0, i)),
        ],
        out_shape=[
            jax.ShapeDtypeStruct((T, D_MODEL), F32),
            jax.ShapeDtypeStruct((T, D_MODEL), BF16),
            jax.ShapeDtypeStruct((nsc, T), F32),
        ],
        compiler_params=_cparams(("parallel",)),
        name="mix",
    )(x2d, oda, ona, wo_bf, g2, wqt_bf, sk_bf)


def _top16_rows(s, nrows):
    L = s.shape[1]
    rid = lax.broadcasted_iota(jnp.int32, (nrows, L), 0).astype(F32)
    slot = lax.broadcasted_iota(jnp.int32, (PEER_TOPK, L), 0)
    vals = jnp.zeros((PEER_TOPK, L), F32)
    idxs = jnp.zeros((PEER_TOPK, L), F32)
    for it in range(PEER_TOPK):
        m = jnp.max(s, axis=0, keepdims=True)
        ix = jnp.min(jnp.where(s == m, rid, float(nrows)), axis=0, keepdims=True)
        vals = jnp.where(slot == it, m, vals)
        idxs = jnp.where(slot == it, ix, idxs)
        s = jnp.where(rid == ix, -jnp.inf, s)
    return vals, idxs


def _topk_kernel(st_ref, hi_ref, lo_ref, g_ref):
    L = st_ref.shape[1]
    ncand = PEER_TOPK + (PEER_TOPK - 1) * 8
    r = lax.broadcasted_iota(jnp.int32, (ncand, L), 0)
    pos = jnp.where(r < PEER_TOPK, r, (1 + ((r - PEER_TOPK) >> 3)) * PEER_TOPK + ((r - PEER_TOPK) & 7)).astype(F32)
    slot = lax.broadcasted_iota(jnp.int32, (PEER_TOPK, L), 0)
    his, los, gates = [], [], []
    for h in range(PEER_HEADS):
        s1, i1 = _top16_rows(st_ref[(2 * h) * PEER_KEYS:(2 * h + 1) * PEER_KEYS, :], PEER_KEYS)
        s2, i2 = _top16_rows(st_ref[(2 * h + 1) * PEER_KEYS:(2 * h + 2) * PEER_KEYS, :], PEER_KEYS)
        cand = [s1[0:1, :] + s2]
        chi = [jnp.broadcast_to(i1[0:1, :], (PEER_TOPK, L))]
        clo = [i2]
        for a in range(1, PEER_TOPK):
            cand.append(s1[a:a + 1, :] + s2[0:8, :])
            chi.append(jnp.broadcast_to(i1[a:a + 1, :], (8, L)))
            clo.append(i2[0:8, :])
        cand = jnp.concatenate(cand, axis=0)
        chi = jnp.concatenate(chi, axis=0)
        clo = jnp.concatenate(clo, axis=0)
        top = jnp.zeros((PEER_TOPK, L), F32)
        thi = jnp.zeros((PEER_TOPK, L), F32)
        tlo = jnp.zeros((PEER_TOPK, L), F32)
        for it in range(PEER_TOPK):
            m = jnp.max(cand, axis=0, keepdims=True)
            p = jnp.min(jnp.where(cand == m, pos, 1e9), axis=0, keepdims=True)
            sel = pos == p
            ehi = jnp.max(jnp.where(sel, chi, -1.0), axis=0, keepdims=True)
            elo = jnp.max(jnp.where(sel, clo, -1.0), axis=0, keepdims=True)
            top = jnp.where(slot == it, m, top)
            thi = jnp.where(slot == it, ehi, thi)
            tlo = jnp.where(slot == it, elo, tlo)
            cand = jnp.where(sel, -jnp.inf, cand)
        e = jnp.exp(top - top[0:1, :])
        his.append(thi)
        los.append(tlo)
        gates.append(e / jnp.sum(e, axis=0, keepdims=True))
    hi_ref[...] = jnp.concatenate(his, axis=0).T
    lo_ref[...] = jnp.concatenate(los, axis=0).T
    g_ref[...] = jnp.concatenate(gates, axis=0).T


def _topk(st, L=128):
    nsc, T = st.shape
    nsel = PEER_HEADS * PEER_TOPK
    out = jax.ShapeDtypeStruct((T, nsel), F32)
    return pl.pallas_call(
        _topk_kernel,
        grid=(T // L,),
        in_specs=[pl.BlockSpec((nsc, L), lambda i: (0, i))],
        out_specs=[pl.BlockSpec((L, nsel), lambda i: (i, 0))] * 3,
        out_shape=[out, out, out],
        compiler_params=_cparams(("parallel",)),
        name="peer_topk",
    )(st)


PEER_PITCH = 136
PEER_EBLK = 256


def _gelu_tanh(x):
    return 0.5 * x * (1.0 + jnp.tanh(math.sqrt(2.0 / math.pi) * (x + 0.044715 * (x * x * x))))


def _peer_kernel(x1_ref, xn_ref, hi_ref, lo_ref, g_ref, u_ref, v_ref, y_ref, gs_ref, acc_ref, *, tm):
    j = pl.program_id(1)
    nsel = PEER_HEADS * PEER_TOPK

    @pl.when(j == 0)
    def _build_gates():
        acc_ref[...] = jnp.zeros_like(acc_ref)
        rid = lax.broadcasted_iota(jnp.int32, (PEER_KEYS, nsel), 0).astype(F32)

        def tok(t, carry):
            hi = hi_ref[pl.ds(t, 1), :]
            lo = lo_ref[pl.ds(t, 1), :]
            g = g_ref[pl.ds(t, 1), :]
            qt = jnp.where(rid == hi, 1.0, 0.0).astype(BF16)
            pt = jnp.where(rid == lo, g, 0.0).astype(BF16)
            gt = lax.dot_general(qt, pt, (((1,), (1,)), ((), ())), preferred_element_type=F32)
            row0 = pl.multiple_of(t * PEER_PITCH, 8)
            gs_ref[pl.ds(row0, PEER_KEYS), :] = gt
            return carry

        lax.fori_loop(0, tm, tok, 0)

    xn = xn_ref[...]
    hmat = lax.dot_general(xn, u_ref[...], (((1,), (1,)), ((), ())), preferred_element_type=F32)
    parts = []
    for c in range(PEER_EBLK // PEER_KEYS):
        i1 = j * (PEER_EBLK // PEER_KEYS) + c
        gb = gs_ref[pl.ds(i1, tm, stride=PEER_PITCH), :]
        parts.append(gb * _gelu_tanh(hmat[:, c * PEER_KEYS:(c + 1) * PEER_KEYS]))
    w = jnp.concatenate(parts, axis=1).astype(BF16)
    acc_ref[...] += jnp.dot(w, v_ref[...], preferred_element_type=F32)

    @pl.when(j == pl.num_programs(1) - 1)
    def _finish():
        y_ref[...] = x1_ref[...] + acc_ref[...]


def _peer(x1, xn, hi, lo, g, u_bf, v_bf, tm=256):
    T = x1.shape[0]
    nsel = PEER_HEADS * PEER_TOPK
    kern = functools.partial(_peer_kernel, tm=tm)
    return pl.pallas_call(
        kern,
        grid=(T // tm, PEER_EXPERTS // PEER_EBLK),
        in_specs=[
            pl.BlockSpec((tm, D_MODEL), lambda i, j: (i, 0)),
            pl.BlockSpec((tm, D_MODEL), lambda i, j: (i, 0)),
            pl.BlockSpec((tm, nsel), lambda i, j: (i, 0)),
            pl.BlockSpec((tm, nsel), lambda i, j: (i, 0)),
            pl.BlockSpec((tm, nsel), lambda i, j: (i, 0)),
            pl.BlockSpec((PEER_EBLK, D_MODEL), lambda i, j: (j, 0)),
            pl.BlockSpec((PEER_EBLK, D_MODEL), lambda i, j: (j, 0)),
        ],
        out_specs=pl.BlockSpec((tm, D_MODEL), lambda i, j: (i, 0)),
        out_shape=jax.ShapeDtypeStruct((T, D_MODEL), F32),
        scratch_shapes=[
            pltpu.VMEM((tm * PEER_PITCH, LANES), F32),
            pltpu.VMEM((tm, D_MODEL), F32),
        ],
        compiler_params=_cparams(("parallel", "arbitrary")),
        name="peer_mlp",
    )(x1, xn, hi, lo, g, u_bf, v_bf)


def _prepare_params(layer_idx, ln1_g, w_in, da_q_norm_g, da_k_norm_g, lam_q1, lam_k1, lam_q2, lam_k2,
                    da_subln_g, na_q_norm_g, na_k_norm_g, na_rpb, w_out, ln2_g,
                    peer_w_query, peer_sub_keys, peer_u, peer_v):
    scale = HEAD_DIM ** -0.5
    gn = jnp.stack([jnp.tile(da_q_norm_g, 8) * scale, jnp.tile(da_k_norm_g, 8),
                    jnp.tile(na_q_norm_g, 8) * scale, jnp.tile(na_k_norm_g, 8)]).astype(F32)
    grp = np.arange(512) // HEAD_DIM
    gm = jnp.asarray((grp[:, None] == grp[None, :]).astype(np.float32) / HEAD_DIM, BF16)
    lambda_init = 0.8 - 0.6 * math.exp(-0.3 * layer_idx)
    lam = (jnp.exp(jnp.sum(lam_q1.astype(F32) * lam_k1.astype(F32)))
           - jnp.exp(jnp.sum(lam_q2.astype(F32) * lam_k2.astype(F32))) + lambda_init)
    slopes = jnp.asarray([2.0 ** (-8.0 * (h + 1) / DA_HEADS) for h in range(DA_HEADS)], F32)
    scal = jnp.concatenate([slopes, lam.reshape(1), jnp.full((1,), 1.0 - lambda_init, F32)])
    return dict(
        g1=ln1_g.reshape(1, D_MODEL), w_in=w_in.astype(BF16), gn=gn, gm=gm, scal=scal,
        subln=da_subln_g.reshape(1, LANES), rpb=na_rpb, w_out=w_out.astype(BF16),
        g2=ln2_g.reshape(1, D_MODEL), wqt=peer_w_query.T.astype(BF16),
        sk=peer_sub_keys.reshape(2 * PEER_HEADS, PEER_KEYS, LANES).astype(BF16),
        u=peer_u.astype(BF16), v=peer_v.astype(BF16))


def _encoder_layer(x, p, tables):
    B, S, _ = x.shape
    x2d = x.reshape(B * S, D_MODEL)
    qkv = _inproj(x2d, p["g1"], p["w_in"], p["gn"], p["gm"])
    oda = _diff_attention(qkv, p["scal"], p["subln"], B, S)
    ona = _neighborhood_attention(qkv, tables, B, S)
    x1, xn, st = _mix(x2d, oda, ona, p["w_out"], p["g2"], p["wqt"], p["sk"])
    hi, lo, g = _topk(st)
    y = _peer(x1, xn, hi, lo, g, p["u"], p["v"])
    return y.reshape(B, S, D_MODEL)


def kernel(x_prompt, x_sample, ln1_g, w_in, da_q_norm_g, da_k_norm_g, da_lambda_q1, da_lambda_k1,
           da_lambda_q2, da_lambda_k2, da_subln_g, na_q_norm_g, na_k_norm_g, na_rpb, w_out, ln2_g,
           peer_w_query, peer_sub_keys, peer_u, peer_v):
    hp, hs = x_prompt, x_sample
    for l in range(ln1_g.shape[0]):
        p = _prepare_params(l, ln1_g[l], w_in[l], da_q_norm_g[l], da_k_norm_g[l], da_lambda_q1[l],
                            da_lambda_k1[l], da_lambda_q2[l], da_lambda_k2[l], da_subln_g[l],
                            na_q_norm_g[l], na_k_norm_g[l], na_rpb[l], w_out[l], ln2_g[l],
                            peer_w_query[l], peer_sub_keys[l], peer_u[l], peer_v[l])
        outs = []
        for x in (hp, hs):
            tables = _na_bias_tables(p["rpb"], x.shape[1] // GRID_W)
            outs.append(_encoder_layer(x, p, tables))
        hp, hs = outs
    return (hp, hs)
```

```python
import functools
import math

import numpy as np
import jax
import jax.numpy as jnp
from jax import lax
from jax.experimental import pallas as pl
from jax.experimental.pallas import tpu as pltpu

F32 = jnp.float32
BF16 = jnp.bfloat16

D_MODEL = 1024
HEAD_DIM = 64
LANES = 128
MXU_N = 256
DA_HEADS = 4
DA_WIDTH = 512
NA_HEADS = 8
NA_WIDTH = 512
GRID_W = 64
NA_WIN_R = 8
NA_WIN_C = 16
IN_COLS = 3 * DA_WIDTH + 3 * NA_WIDTH
PEER_HEADS = 8
PEER_KEYS = 128
PEER_QDIM = 256
PEER_TOPK = 16
PEER_EXPERTS = PEER_KEYS * PEER_KEYS
RMS_EPS = 1e-6
NEG = -1e30
VMEM_LIMIT = 56 * 1024 * 1024

_DQ, _DK, _DV = 0, 4, 8
_NQ, _NK, _NV = 12, 16, 20


def _cparams(sem):
    return pltpu.CompilerParams(dimension_semantics=sem, vmem_limit_bytes=VMEM_LIMIT)


def _group_ms(x, gm):
    sq = x * x
    hi = sq.astype(BF16)
    lo = (sq - hi.astype(F32)).astype(BF16)
    return (jnp.dot(hi, gm, preferred_element_type=F32)
            + jnp.dot(lo, gm, preferred_element_type=F32))


def _inproj_kernel(x_ref, g1_ref, w_ref, gn_ref, gm_ref, o_ref):
    x = x_ref[...]
    ms = jnp.mean(x * x, axis=-1, keepdims=True)
    xn = (x * lax.rsqrt(ms + RMS_EPS) * g1_ref[...]).astype(BF16)
    proj = jnp.dot(xn, w_ref[...], preferred_element_type=F32)
    gm = gm_ref[...]
    for sec, row in ((0, 0), (1, 1), (3, 2), (4, 3)):
        xs = proj[:, sec * 512:(sec + 1) * 512]
        y = xs * lax.rsqrt(_group_ms(xs, gm) + RMS_EPS) * gn_ref[row:row + 1, :]
        o_ref[:, sec * 512:(sec + 1) * 512] = y.astype(BF16)
    for sec in (2, 5):
        o_ref[:, sec * 512:(sec + 1) * 512] = proj[:, sec * 512:(sec + 1) * 512].astype(BF16)


def _inproj(x2d, g1, w_in_bf, gn, gm, tm=256):
    T = x2d.shape[0]
    return pl.pallas_call(
        _inproj_kernel,
        grid=(T // tm,),
        in_specs=[
            pl.BlockSpec((tm, D_MODEL), lambda i: (i, 0)),
            pl.BlockSpec((1, D_MODEL), lambda i: (0, 0)),
            pl.BlockSpec((D_MODEL, IN_COLS), lambda i: (0, 0)),
            pl.BlockSpec((4, 512), lambda i: (0, 0)),
            pl.BlockSpec((512, 512), lambda i: (0, 0)),
        ],
        out_specs=pl.BlockSpec((tm, IN_COLS), lambda i: (i, 0)),
        out_shape=jax.ShapeDtypeStruct((T, IN_COLS), BF16),
        compiler_params=_cparams(("parallel",)),
        name="inproj",
    )(x2d, g1, w_in_bf, gn, gm)


DA_TQ = 512
DA_TK = 512


def _da_kernel(sc_ref, q_ref, k_ref, v_ref, g_ref, db_ref, o_ref,
               qst_ref, vt_ref, m_ref, l_ref, acc_ref, sta_ref, stb_ref, *, nk):
    tq, tk = DA_TQ, DA_TK
    h = pl.program_id(1)
    qi = pl.program_id(2)
    slope = sc_ref[h]
    lam = sc_ref[4]
    post = sc_ref[5]

    @pl.when(qi == 0)
    def _transpose_values():
        for c in range(nk):
            vt_ref[c] = v_ref[c * tk:(c + 1) * tk, :].astype(F32).T.astype(BF16)

    qT = q_ref[...].astype(F32).T
    row = lax.broadcasted_iota(jnp.int32, (LANES, tq), 0)
    qst_ref[0:LANES, 0:tq] = jnp.where(row < HEAD_DIM, qT, 0.0).astype(BF16)
    qst_ref[0:LANES, tq:2 * tq] = jnp.where(row >= HEAD_DIM, qT, 0.0).astype(BF16)
    row2 = lax.broadcasted_iota(jnp.int32, (LANES, 2 * tq), 0)
    qst_ref[LANES:2 * LANES, :] = jnp.where(row2 < 2, 1.0, 0.0).astype(BF16)
    m_ref[...] = jnp.full((1, 2 * tq), NEG, F32)
    l_ref[...] = jnp.zeros((1, 2 * tq), F32)
    acc_ref[...] = jnp.zeros((LANES, 2 * tq), F32)

    j = lax.broadcasted_iota(jnp.int32, (tk, LANES), 0)
    lane = lax.broadcasted_iota(jnp.int32, (tk, LANES), 1)
    pos = jnp.where(lane == 0, (j >> 5).astype(F32) * (32.0 * slope),
                    jnp.where(lane == 1, (j & 31).astype(F32) * slope, 0.0))
    col = lax.broadcasted_iota(jnp.int32, (1, 2 * tq), 1)
    icol = jnp.where(col >= tq, col - tq, col).astype(F32)

    nch = 2 * tq // MXU_N

    def cols(n):
        return slice(n * MXU_N, (n + 1) * MXU_N)

    def side(blk):
        return jnp.where(qi > blk, 1.0, jnp.where(qi < blk, -1.0, 0.0))

    def key_operand(blk, sgn):
        k0 = pl.multiple_of(blk * tk, tk)
        return jnp.concatenate([k_ref[pl.ds(k0, tk), :], (pos * sgn).astype(BF16)], axis=1)

    def softmax_pv(n, src_ref, vtb, cq):
        sl = cols(n)
        st = src_ref[:, sl]
        c_n = cq[:, sl]
        m_old = m_ref[:, sl]
        m_new = jnp.maximum(m_old, jnp.max(st, axis=0, keepdims=True) + c_n)
        p = jnp.exp(st - (m_new - c_n))
        alpha = jnp.exp(m_old - m_new)
        l_ref[:, sl] = alpha * l_ref[:, sl] + jnp.sum(p, axis=0, keepdims=True)
        acc_ref[:, sl] = alpha * acc_ref[:, sl] + jnp.dot(vtb, p.astype(BF16), preferred_element_type=F32)
        m_ref[:, sl] = m_new

    def step(cur, nxt, src_ref, dst_ref):
        ka = key_operand(nxt, side(nxt))
        vtb = vt_ref[cur]
        cq = (-side(cur) * slope) * ((qi * tq - cur * tk).astype(F32) + icol)

        def scores(n):
            dst_ref[:, cols(n)] = jnp.dot(ka, qst_ref[:, cols(n)], preferred_element_type=F32)

        scores(0)
        scores(1)
        for n in range(nch):
            softmax_pv(n, src_ref, vtb, cq)
            if n + 2 < nch:
                scores(n + 2)

    def other(x):
        return x + (x >= qi).astype(jnp.int32)

    ka0 = key_operand(qi, 0.0)
    for n in range(nch):
        sta_ref[:, cols(n)] = (jnp.dot(ka0, qst_ref[:, cols(n)], preferred_element_type=F32)
                               + db_ref[0, :, cols(n)])

    def pair(ii, carry):
        i0 = 2 * ii
        cur0 = jnp.where(ii == 0, qi, other(i0 - 1))
        nxt0 = other(i0)
        step(cur0, nxt0, sta_ref, stb_ref)
        step(nxt0, other(jnp.minimum(i0 + 1, nk - 2)), stb_ref, sta_ref)
        return carry

    lax.fori_loop(0, nk // 2, pair, 0)

    on = acc_ref[...] / l_ref[...]
    oT = on[:, 0:tq] - lam * on[:, tq:2 * tq]
    ms = jnp.mean(oT * oT, axis=0, keepdims=True)
    oT = oT * lax.rsqrt(ms + RMS_EPS) * g_ref[...] * post
    o_ref[...] = oT.T.astype(BF16)


def _da_diag_bias(slopes):
    jj = np.arange(DA_TK)[:, None]
    ii = np.concatenate([np.arange(DA_TQ), np.arange(DA_TQ)])[None, :]
    dist = jnp.asarray(np.abs(ii - jj).astype(np.float32))
    return -slopes[:, None, None] * dist[None]


def _diff_attention(qkv, scal, subln_col, dbias, B, S):
    T = B * S
    tq, tk = DA_TQ, DA_TK
    nq, nk = S // tq, S // tk
    kern = functools.partial(_da_kernel, nk=nk)
    return pl.pallas_call(
        kern,
        grid=(B, DA_HEADS, nq),
        in_specs=[
            pl.BlockSpec(memory_space=pltpu.SMEM),
            pl.BlockSpec((tq, LANES), lambda b, h, i: (b * nq + i, _DQ + h)),
            pl.BlockSpec((S, LANES), lambda b, h, i: (b, _DK + h)),
            pl.BlockSpec((S, LANES), lambda b, h, i: (b, _DV + h)),
            pl.BlockSpec((LANES, 1), lambda b, h, i: (0, 0)),
            pl.BlockSpec((1, tk, 2 * tq), lambda b, h, i: (h, 0, 0)),
        ],
        out_specs=pl.BlockSpec((tq, LANES), lambda b, h, i: (b * nq + i, h)),
        out_shape=jax.ShapeDtypeStruct((T, DA_WIDTH), BF16),
        scratch_shapes=[
            pltpu.VMEM((2 * LANES, 2 * tq), BF16),
            pltpu.VMEM((nk, LANES, tk), BF16),
            pltpu.VMEM((1, 2 * tq), F32),
            pltpu.VMEM((1, 2 * tq), F32),
            pltpu.VMEM((LANES, 2 * tq), F32),
            pltpu.VMEM((tk, 2 * tq), F32),
            pltpu.VMEM((tk, 2 * tq), F32),
        ],
        compiler_params=_cparams(("parallel", "parallel", "arbitrary")),
        name="diff_attn",
    )(scal, qkv, qkv, qkv, subln_col, dbias)


NA_QROWS = 8
NA_KROWS = 16
NA_TQ = NA_QROWS * GRID_W
NA_TK = NA_KROWS * GRID_W


def _na_bias_tables(rpb, rows):
    cols = np.arange(GRID_W)
    cstart = np.clip(cols - NA_WIN_C // 2, 0, GRID_W - NA_WIN_C)
    kc = cols[None, :]
    col_ok = (kc >= cstart[:, None]) & (kc < cstart[:, None] + NA_WIN_C)
    dc = np.clip(kc - cols[:, None] + (NA_WIN_C - 1), 0, 2 * NA_WIN_C - 2)
    t = jnp.where(jnp.asarray(col_ok)[None, None], rpb[:, :, dc], NEG)
    t = jnp.concatenate([t, jnp.full((NA_HEADS, 1, GRID_W, GRID_W), NEG, F32)], axis=1)
    tables = []
    for r0, w0 in ((0, 0), (rows // 2 // NA_QROWS * NA_QROWS, None), (rows - NA_QROWS, rows - NA_KROWS)):
        if w0 is None:
            r0 = max(NA_QROWS, min(r0, rows - 2 * NA_QROWS))
            w0 = r0 - NA_WIN_R // 2
        r = r0 + np.arange(NA_QROWS)[:, None]
        kr = w0 + np.arange(NA_KROWS)[None, :]
        rs = np.clip(r - NA_WIN_R // 2, 0, rows - NA_WIN_R)
        ok = (kr >= rs) & (kr < rs + NA_WIN_R)
        dr = np.where(ok, kr - r + (NA_WIN_R - 1), 2 * NA_WIN_R - 1)
        tb = t[:, dr]
        tables.append(jnp.transpose(tb, (0, 1, 3, 2, 4)).reshape(NA_HEADS, NA_TQ, NA_TK))
    return jnp.stack(tables)


def _na_kernel(q_ref, k_ref, v_ref, b_ref, o_ref, *, rows):
    j = pl.program_id(2)
    w0 = jnp.clip(j * NA_QROWS - NA_WIN_R // 2, 0, rows - NA_KROWS)
    start = pl.multiple_of(w0 * GRID_W, GRID_W)
    kw = k_ref[pl.ds(start, NA_TK), :]
    vw = v_ref[pl.ds(start, NA_TK), :]
    q = q_ref[...]
    lane = lax.broadcasted_iota(jnp.int32, (NA_TQ, LANES), 1)
    zero = jnp.zeros_like(q)
    outs = []
    for hh in range(2):
        msk = (lane < HEAD_DIM) if hh == 0 else (lane >= HEAD_DIM)
        qm = jnp.where(msk, q, zero)
        s = lax.dot_general(qm, kw, (((1,), (1,)), ((), ())), preferred_element_type=F32)
        s = s + b_ref[0, hh]
        m = jnp.max(s, axis=-1, keepdims=True)
        p = jnp.exp(s - m)
        l = jnp.sum(p, axis=-1, keepdims=True)
        outs.append(jnp.dot(p.astype(BF16), vw, preferred_element_type=F32) / l)
    o_ref[...] = jnp.where(lane < HEAD_DIM, outs[0], outs[1]).astype(BF16)


def _neighborhood_attention(qkv, tables, B, S):
    T = B * S
    rows = S // GRID_W
    nblk = rows // NA_QROWS
    kern = functools.partial(_na_kernel, rows=rows)

    def btype(j):
        return jnp.where(j == 0, 0, jnp.where(j == nblk - 1, 2, 1))

    return pl.pallas_call(
        kern,
        grid=(NA_HEADS // 2, B, nblk),
        in_specs=[
            pl.BlockSpec((NA_TQ, LANES), lambda p, b, j: (b * nblk + j, _NQ + p)),
            pl.BlockSpec((S, LANES), lambda p, b, j: (b, _NK + p)),
            pl.BlockSpec((S, LANES), lambda p, b, j: (b, _NV + p)),
            pl.BlockSpec((1, 2, NA_TQ, NA_TK), lambda p, b, j: (btype(j), p, 0, 0)),
        ],
        out_specs=pl.BlockSpec((NA_TQ, LANES), lambda p, b, j: (b * nblk + j, p)),
        out_shape=jax.ShapeDtypeStruct((T, NA_WIDTH), BF16),
        compiler_params=_cparams(("parallel", "parallel", "parallel")),
        name="nbr_attn",
    )(qkv, qkv, qkv, tables)


def _mix_kernel(x_ref, oda_ref, ona_ref, wo_ref, g2_ref, wqt_ref, sk_ref, x1_ref, xn_ref, st_ref):
    x1 = (x_ref[...]
          + jnp.dot(oda_ref[...], wo_ref[0:DA_WIDTH, :], preferred_element_type=F32)
          + jnp.dot(ona_ref[...], wo_ref[DA_WIDTH:, :], preferred_element_type=F32))
    x1_ref[...] = x1
    ms = jnp.mean(x1 * x1, axis=-1, keepdims=True)
    xn = (x1 * lax.rsqrt(ms + RMS_EPS) * g2_ref[...]).astype(BF16)
    xn_ref[...] = xn
    qt = lax.dot_general(wqt_ref[...], xn, (((1,), (1,)), ((), ())), preferred_element_type=F32).astype(BF16)
    for hp in range(2 * PEER_HEADS):
        st_ref[hp * PEER_KEYS:(hp + 1) * PEER_KEYS, :] = jnp.dot(
            sk_ref[hp], qt[hp * LANES:(hp + 1) * LANES, :], preferred_element_type=F32)


def _mix(x2d, oda, ona, wo_bf, g2, wqt_bf, sk_bf, tm=256):
    T = x2d.shape[0]
    nsc = 2 * PEER_HEADS * PEER_KEYS
    return pl.pallas_call(
        _mix_kernel,
        grid=(T // tm,),
        in_specs=[
            pl.BlockSpec((tm, D_MODEL), lambda i: (i, 0)),
            pl.BlockSpec((tm, DA_WIDTH), lambda i: (i, 0)),
            pl.BlockSpec((tm, NA_WIDTH), lambda i: (i, 0)),
            pl.BlockSpec((D_MODEL, D_MODEL), lambda i: (0, 0)),
            pl.BlockSpec((1, D_MODEL), lambda i: (0, 0)),
            pl.BlockSpec((nsc, D_MODEL), lambda i: (0, 0)),
            pl.BlockSpec((2 * PEER_HEADS, PEER_KEYS, LANES), lambda i: (0, 0, 0)),
        ],
        out_specs=[
            pl.BlockSpec((tm, D_MODEL), lambda i: (i, 0)),
            pl.BlockSpec((tm, D_MODEL), lambda i: (i, 0)),
            pl.BlockSpec((nsc, tm), lambda i: (0, i)),
        ],
        out_shape=[
            jax.ShapeDtypeStruct((T, D_MODEL), F32),
            jax.ShapeDtypeStruct((T, D_MODEL), BF16),
            jax.ShapeDtypeStruct((nsc, T), F32),
        ],
        compiler_params=_cparams(("parallel",)),
        name="mix",
    )(x2d, oda, ona, wo_bf, g2, wqt_bf, sk_bf)


def _top16_rows(s, nrows):
    L = s.shape[1]
    rid = lax.broadcasted_iota(jnp.int32, (nrows, L), 0).astype(F32)
    slot = lax.broadcasted_iota(jnp.int32, (PEER_TOPK, L), 0)
    vals = jnp.zeros((PEER_TOPK, L), F32)
    idxs = jnp.zeros((PEER_TOPK, L), F32)
    for it in range(PEER_TOPK):
        m = jnp.max(s, axis=0, keepdims=True)
        ix = jnp.min(jnp.where(s == m, rid, float(nrows)), axis=0, keepdims=True)
        vals = jnp.where(slot == it, m, vals)
        idxs = jnp.where(slot == it, ix, idxs)
        s = jnp.where(rid == ix, -jnp.inf, s)
    return vals, idxs


def _topk_kernel(st_ref, hi_ref, lo_ref, g_ref):
    L = st_ref.shape[1]
    ncand = PEER_TOPK + (PEER_TOPK - 1) * 8
    r = lax.broadcasted_iota(jnp.int32, (ncand, L), 0)
    pos = jnp.where(r < PEER_TOPK, r, (1 + ((r - PEER_TOPK) >> 3)) * PEER_TOPK + ((r - PEER_TOPK) & 7)).astype(F32)
    slot = lax.broadcasted_iota(jnp.int32, (PEER_TOPK, L), 0)
    his, los, gates = [], [], []
    for h in range(PEER_HEADS):
        s1, i1 = _top16_rows(st_ref[(2 * h) * PEER_KEYS:(2 * h + 1) * PEER_KEYS, :], PEER_KEYS)
        s2, i2 = _top16_rows(st_ref[(2 * h + 1) * PEER_KEYS:(2 * h + 2) * PEER_KEYS, :], PEER_KEYS)
        cand = [s1[0:1, :] + s2]
        chi = [jnp.broadcast_to(i1[0:1, :], (PEER_TOPK, L))]
        clo = [i2]
        for a in range(1, PEER_TOPK):
            cand.append(s1[a:a + 1, :] + s2[0:8, :])
            chi.append(jnp.broadcast_to(i1[a:a + 1, :], (8, L)))
            clo.append(i2[0:8, :])
        cand = jnp.concatenate(cand, axis=0)
        chi = jnp.concatenate(chi, axis=0)
        clo = jnp.concatenate(clo, axis=0)
        top = jnp.zeros((PEER_TOPK, L), F32)
        thi = jnp.zeros((PEER_TOPK, L), F32)
        tlo = jnp.zeros((PEER_TOPK, L), F32)
        for it in range(PEER_TOPK):
            m = jnp.max(cand, axis=0, keepdims=True)
            p = jnp.min(jnp.where(cand == m, pos, 1e9), axis=0, keepdims=True)
            sel = pos == p
            ehi = jnp.max(jnp.where(sel, chi, -1.0), axis=0, keepdims=True)
            elo = jnp.max(jnp.where(sel, clo, -1.0), axis=0, keepdims=True)
            top = jnp.where(slot == it, m, top)
            thi = jnp.where(slot == it, ehi, thi)
            tlo = jnp.where(slot == it, elo, tlo)
            cand = jnp.where(sel, -jnp.inf, cand)
        e = jnp.exp(top - top[0:1, :])
        his.append(thi)
        los.append(tlo)
        gates.append(e / jnp.sum(e, axis=0, keepdims=True))
    hi_ref[...] = jnp.concatenate(his, axis=0).T
    lo_ref[...] = jnp.concatenate(los, axis=0).T
    g_ref[...] = jnp.concatenate(gates, axis=0).T


def _topk(st, L=128):
    nsc, T = st.shape
    nsel = PEER_HEADS * PEER_TOPK
    out = jax.ShapeDtypeStruct((T, nsel), F32)
    return pl.pallas_call(
        _topk_kernel,
        grid=(T // L,),
        in_specs=[pl.BlockSpec((nsc, L), lambda i: (0, i))],
        out_specs=[pl.BlockSpec((L, nsel), lambda i: (i, 0))] * 3,
        out_shape=[out, out, out],
        compiler_params=_cparams(("parallel",)),
        name="peer_topk",
    )(st)


GATE_GRP = 16
GATE_PITCH = 136


def _gates_kernel(hi_ref, lo_ref, g_ref, o_ref, gs_ref, *, tg):
    nsel = PEER_HEADS * PEER_TOPK
    rid = lax.broadcasted_iota(jnp.int32, (PEER_KEYS, nsel), 0).astype(F32)

    def group(gi, carry):
        t0 = pl.multiple_of(gi * GATE_GRP, GATE_GRP)
        hi16 = hi_ref[pl.ds(t0, GATE_GRP), :]
        lo16 = lo_ref[pl.ds(t0, GATE_GRP), :]
        g16 = g_ref[pl.ds(t0, GATE_GRP), :]
        for tt in range(GATE_GRP):
            qt = jnp.where(rid == hi16[tt:tt + 1, :], 1.0, 0.0).astype(BF16)
            pt = jnp.where(rid == lo16[tt:tt + 1, :], g16[tt:tt + 1, :], 0.0).astype(BF16)
            gs_ref[tt * GATE_PITCH:tt * GATE_PITCH + PEER_KEYS, :] = lax.dot_general(
                qt, pt, (((1,), (1,)), ((), ())), preferred_element_type=F32)
        for i1 in range(PEER_KEYS):
            rows = gs_ref[pl.ds(i1, GATE_GRP, stride=GATE_PITCH), :]
            o_ref[i1, pl.ds(t0, GATE_GRP), :] = rows.astype(BF16)
        return carry

    lax.fori_loop(0, tg // GATE_GRP, group, 0)


def _gates(hi, lo, g, tg=256):
    T, nsel = hi.shape
    kern = functools.partial(_gates_kernel, tg=tg)
    return pl.pallas_call(
        kern,
        grid=(T // tg,),
        in_specs=[pl.BlockSpec((tg, nsel), lambda i: (i, 0))] * 3,
        out_specs=pl.BlockSpec((PEER_KEYS, tg, PEER_KEYS), lambda i: (0, i, 0)),
        out_shape=jax.ShapeDtypeStruct((PEER_KEYS, T, PEER_KEYS), BF16),
        scratch_shapes=[pltpu.VMEM((GATE_GRP * GATE_PITCH, LANES), F32)],
        compiler_params=_cparams(("parallel",)),
        name="peer_gates",
    )(hi, lo, g)


PEER_TM = 1024
PEER_EBLK = 512


def _gelu_tanh(x):
    return 0.5 * x * (1.0 + jnp.tanh(math.sqrt(2.0 / math.pi) * (x + 0.044715 * (x * x * x))))


def _peer_kernel(x1_ref, xn_ref, ut_ref, v_ref, g3_ref, y_ref):
    j = pl.program_id(1)

    @pl.when(j == 0)
    def _residual():
        y_ref[...] = x1_ref[...]

    hmat = jnp.dot(xn_ref[...], ut_ref[...], preferred_element_type=F32)
    gates = jnp.concatenate([g3_ref[c] for c in range(PEER_EBLK // PEER_KEYS)], axis=1)
    w = (gates.astype(F32) * _gelu_tanh(hmat)).astype(BF16)
    y_ref[...] += jnp.dot(w, v_ref[...], preferred_element_type=F32)


def _peer(x1, xn, ut_bf, v_bf, g3):
    T = x1.shape[0]
    tm = PEER_TM
    return pl.pallas_call(
        _peer_kernel,
        grid=(T // tm, PEER_EXPERTS // PEER_EBLK),
        in_specs=[
            pl.BlockSpec((tm, D_MODEL), lambda i, j: (i, 0)),
            pl.BlockSpec((tm, D_MODEL), lambda i, j: (i, 0)),
            pl.BlockSpec((D_MODEL, PEER_EBLK), lambda i, j: (0, j)),
            pl.BlockSpec((PEER_EBLK, D_MODEL), lambda i, j: (j, 0)),
            pl.BlockSpec((PEER_EBLK // PEER_KEYS, tm, PEER_KEYS), lambda i, j: (j, i, 0)),
        ],
        out_specs=pl.BlockSpec((tm, D_MODEL), lambda i, j: (i, 0)),
        out_shape=jax.ShapeDtypeStruct((T, D_MODEL), F32),
        compiler_params=_cparams(("parallel", "arbitrary")),
        name="peer_mlp",
    )(x1, xn, ut_bf, v_bf, g3)


def _prepare_params(layer_idx, ln1_g, w_in, da_q_norm_g, da_k_norm_g, lam_q1, lam_k1, lam_q2, lam_k2,
                    da_subln_g, na_q_norm_g, na_k_norm_g, na_rpb, w_out, ln2_g,
                    peer_w_query, peer_sub_keys, peer_u, peer_v):
    scale = HEAD_DIM ** -0.5
    gn = jnp.stack([jnp.tile(da_q_norm_g, 8) * scale, jnp.tile(da_k_norm_g, 8),
                    jnp.tile(na_q_norm_g, 8) * scale, jnp.tile(na_k_norm_g, 8)]).astype(F32)
    grp = np.arange(512) // HEAD_DIM
    gm = jnp.asarray((grp[:, None] == grp[None, :]).astype(np.float32) / HEAD_DIM, BF16)
    lambda_init = 0.8 - 0.6 * math.exp(-0.3 * layer_idx)
    lam = (jnp.exp(jnp.sum(lam_q1.astype(F32) * lam_k1.astype(F32)))
           - jnp.exp(jnp.sum(lam_q2.astype(F32) * lam_k2.astype(F32))) + lambda_init)
    slopes = jnp.asarray([2.0 ** (-8.0 * (h + 1) / DA_HEADS) for h in range(DA_HEADS)], F32)
    scal = jnp.concatenate([slopes, lam.reshape(1), jnp.full((1,), 1.0 - lambda_init, F32)])
    return dict(
        g1=ln1_g.reshape(1, D_MODEL), w_in=w_in.astype(BF16), gn=gn, gm=gm, scal=scal,
        subln=da_subln_g.reshape(LANES, 1), dbias=_da_diag_bias(slopes), rpb=na_rpb,
        w_out=w_out.astype(BF16), g2=ln2_g.reshape(1, D_MODEL), wqt=peer_w_query.T.astype(BF16),
        sk=peer_sub_keys.reshape(2 * PEER_HEADS, PEER_KEYS, LANES).astype(BF16),
        ut=peer_u.T.astype(BF16), v=peer_v.astype(BF16))


def _encoder_layer(x, p, tables):
    B, S, _ = x.shape
    x2d = x.reshape(B * S, D_MODEL)
    qkv = _inproj(x2d, p["g1"], p["w_in"], p["gn"], p["gm"])
    oda = _diff_attention(qkv, p["scal"], p["subln"], p["dbias"], B, S)
    ona = _neighborhood_attention(qkv, tables, B, S)
    x1, xn, st = _mix(x2d, oda, ona, p["w_out"], p["g2"], p["wqt"], p["sk"])
    hi, lo, g = _topk(st)
    g3 = _gates(hi, lo, g)
    y = _peer(x1, xn, p["ut"], p["v"], g3)
    return y.reshape(B, S, D_MODEL)


def kernel(x_prompt, x_sample, ln1_g, w_in, da_q_norm_g, da_k_norm_g, da_lambda_q1, da_lambda_k1,
           da_lambda_q2, da_lambda_k2, da_subln_g, na_q_norm_g, na_k_norm_g, na_rpb, w_out, ln2_g,
           peer_w_query, peer_sub_keys, peer_u, peer_v):
    hp, hs = x_prompt, x_sample
    for l in range(ln1_g.shape[0]):
        p = _prepare_params(l, ln1_g[l], w_in[l], da_q_norm_g[l], da_k_norm_g[l], da_lambda_q1[l],
                            da_lambda_k1[l], da_lambda_q2[l], da_lambda_k2[l], da_subln_g[l],
                            na_q_norm_g[l], na_k_norm_g[l], na_rpb[l], w_out[l], ln2_g[l],
                            peer_w_query[l], peer_sub_keys[l], peer_u[l], peer_v[l])
        outs = []
        for x in (hp, hs):
            tables = _na_bias_tables(p["rpb"], x.shape[1] // GRID_W)
            outs.append(_encoder_layer(x, p, tables))
        hp, hs = outs
    return (hp, hs)
```

```python
import functools
import math

import numpy as np
import jax
import jax.numpy as jnp
from jax import lax
from jax.experimental import pallas as pl
from jax.experimental.pallas import tpu as pltpu

F32 = jnp.float32
BF16 = jnp.bfloat16

D_MODEL = 1024
HEAD_DIM = 64
LANES = 128
MXU_N = 256
DA_HEADS = 4
DA_WIDTH = 512
NA_HEADS = 8
NA_WIDTH = 512
GRID_W = 64
NA_WIN_R = 8
NA_WIN_C = 16
IN_COLS = 3 * DA_WIDTH + 3 * NA_WIDTH
PEER_HEADS = 8
PEER_KEYS = 128
PEER_QDIM = 256
PEER_TOPK = 16
PEER_EXPERTS = PEER_KEYS * PEER_KEYS
RMS_EPS = 1e-6
NEG = -1e30
VMEM_LIMIT = 56 * 1024 * 1024

_DQ, _DK, _DV = 0, 4, 8
_NQ, _NK, _NV = 12, 16, 20


def _cparams(sem):
    return pltpu.CompilerParams(dimension_semantics=sem, vmem_limit_bytes=VMEM_LIMIT)


def _group_ms(x, gm):
    sq = x * x
    hi = sq.astype(BF16)
    lo = (sq - hi.astype(F32)).astype(BF16)
    return (jnp.dot(hi, gm, preferred_element_type=F32)
            + jnp.dot(lo, gm, preferred_element_type=F32))


def _inproj_kernel(x_ref, g1_ref, w_ref, gn_ref, gm_ref, o_ref):
    x = x_ref[...]
    ms = jnp.mean(x * x, axis=-1, keepdims=True)
    xn = (x * lax.rsqrt(ms + RMS_EPS) * g1_ref[...]).astype(BF16)
    proj = jnp.dot(xn, w_ref[...], preferred_element_type=F32)
    gm = gm_ref[...]
    for sec, row in ((0, 0), (1, 1), (3, 2), (4, 3)):
        xs = proj[:, sec * 512:(sec + 1) * 512]
        y = xs * lax.rsqrt(_group_ms(xs, gm) + RMS_EPS) * gn_ref[row:row + 1, :]
        o_ref[:, sec * 512:(sec + 1) * 512] = y.astype(BF16)
    for sec in (2, 5):
        o_ref[:, sec * 512:(sec + 1) * 512] = proj[:, sec * 512:(sec + 1) * 512].astype(BF16)


def _inproj(x2d, g1, w_in_bf, gn, gm, tm=512):
    T = x2d.shape[0]
    return pl.pallas_call(
        _inproj_kernel,
        grid=(T // tm,),
        in_specs=[
            pl.BlockSpec((tm, D_MODEL), lambda i: (i, 0)),
            pl.BlockSpec((1, D_MODEL), lambda i: (0, 0)),
            pl.BlockSpec((D_MODEL, IN_COLS), lambda i: (0, 0)),
            pl.BlockSpec((4, 512), lambda i: (0, 0)),
            pl.BlockSpec((512, 512), lambda i: (0, 0)),
        ],
        out_specs=pl.BlockSpec((tm, IN_COLS), lambda i: (i, 0)),
        out_shape=jax.ShapeDtypeStruct((T, IN_COLS), BF16),
        compiler_params=_cparams(("parallel",)),
        name="inproj",
    )(x2d, g1, w_in_bf, gn, gm)


DA_TQ = 512
DA_TK = 512
DA_POS_LANES = 6
DA_VROWS = 144
LOG2E = math.log2(math.e)


def _da_kernel(sc_ref, q_ref, k_ref, v_ref, g_ref, db_ref, pos_ref, o_ref,
               qst_ref, vt_ref, m_ref, acc_ref, sta_ref, stb_ref, *, nk):
    tq, tk = DA_TQ, DA_TK
    h = pl.program_id(1)
    qi = pl.program_id(2)
    slope = sc_ref[h]
    lam = sc_ref[4]
    post = sc_ref[5]

    @pl.when(qi == 0)
    def _transpose_values():
        ones_row = jnp.where(lax.broadcasted_iota(jnp.int32, (DA_VROWS - LANES, tk), 0) == 0, 1.0, 0.0)
        for c in range(nk):
            vt_ref[c, 0:LANES, :] = v_ref[c * tk:(c + 1) * tk, :].astype(F32).T.astype(BF16)
            vt_ref[c, LANES:DA_VROWS, :] = ones_row.astype(BF16)

    qT = q_ref[...].astype(F32).T
    row = lax.broadcasted_iota(jnp.int32, (LANES, tq), 0)
    qst_ref[0:LANES, 0:tq] = jnp.where(row < HEAD_DIM, qT, 0.0).astype(BF16)
    qst_ref[0:LANES, tq:2 * tq] = jnp.where(row >= HEAD_DIM, qT, 0.0).astype(BF16)
    row2 = lax.broadcasted_iota(jnp.int32, (LANES, 2 * tq), 0)
    qst_ref[LANES:2 * LANES, :] = jnp.where(row2 < DA_POS_LANES, 1.0, 0.0).astype(BF16)
    m_ref[...] = jnp.full((1, 2 * tq), NEG, F32)
    acc_ref[...] = jnp.zeros((DA_VROWS, 2 * tq), F32)

    pos = pos_ref[0].astype(F32)
    col = lax.broadcasted_iota(jnp.int32, (1, 2 * tq), 1)
    icol = jnp.where(col >= tq, col - tq, col).astype(F32)

    nch = 2 * tq // MXU_N

    def cols(n):
        return slice(n * MXU_N, (n + 1) * MXU_N)

    def side(blk):
        return jnp.where(qi > blk, 1.0, jnp.where(qi < blk, -1.0, 0.0))

    def key_operand(blk, sgn):
        k0 = pl.multiple_of(blk * tk, tk)
        return jnp.concatenate([k_ref[pl.ds(k0, tk), :], (pos * sgn).astype(BF16)], axis=1)

    def softmax_pv(n, src_ref, vtb, cq):
        sl = cols(n)
        st = src_ref[:, sl]
        c_n = cq[:, sl]
        m_old = m_ref[:, sl]
        m_new = jnp.maximum(m_old, jnp.max(st, axis=0, keepdims=True) + c_n)
        p = jnp.exp2(st - (m_new - c_n)).astype(BF16)
        alpha = jnp.exp2(m_old - m_new)
        acc_ref[:, sl] = alpha * acc_ref[:, sl] + jnp.dot(vtb, p, preferred_element_type=F32)
        m_ref[:, sl] = m_new

    def step(cur, nxt, src_ref, dst_ref):
        ka = key_operand(nxt, side(nxt))
        vtb = vt_ref[cur]
        cq = (-side(cur) * slope * LOG2E) * ((qi * tq - cur * tk).astype(F32) + icol)

        def scores(n):
            dst_ref[:, cols(n)] = jnp.dot(ka, qst_ref[:, cols(n)], preferred_element_type=F32)

        scores(0)
        scores(1)
        for n in range(nch):
            softmax_pv(n, src_ref, vtb, cq)
            if n + 2 < nch:
                scores(n + 2)

    def other(x):
        return x + (x >= qi).astype(jnp.int32)

    ka0 = key_operand(qi, 0.0)
    for n in range(nch):
        sta_ref[:, cols(n)] = (jnp.dot(ka0, qst_ref[:, cols(n)], preferred_element_type=F32)
                               + db_ref[0, :, cols(n)])

    unroll = 4 if nk % 4 == 0 else 2

    def trip(ii, carry):
        cur = jnp.where(ii == 0, qi, other(unroll * ii - 1))
        for s in range(unroll):
            nxt = other(jnp.minimum(unroll * ii + s, nk - 2))
            src_ref, dst_ref = (sta_ref, stb_ref) if s % 2 == 0 else (stb_ref, sta_ref)
            step(cur, nxt, src_ref, dst_ref)
            cur = nxt
        return carry

    lax.fori_loop(0, nk // unroll, trip, 0)

    on = acc_ref[0:LANES, :] / acc_ref[LANES:LANES + 1, :]
    oT = on[:, 0:tq] - lam * on[:, tq:2 * tq]
    ms = jnp.mean(oT * oT, axis=0, keepdims=True)
    oT = oT * lax.rsqrt(ms + RMS_EPS) * g_ref[...] * post
    o_ref[...] = oT.T.astype(BF16)


def _da_tables(slopes):
    jj = np.arange(DA_TK)[:, None]
    ii = np.concatenate([np.arange(DA_TQ), np.arange(DA_TQ)])[None, :]
    dist = jnp.asarray(np.abs(ii - jj).astype(np.float32))
    dbias = -(slopes * LOG2E)[:, None, None] * dist[None]
    j = np.arange(DA_TK)
    terms = []
    for part in ((j >> 5) * 32.0, (j & 31) * 1.0):
        rem = (slopes * LOG2E)[:, None] * jnp.asarray(part.astype(np.float32))[None, :]
        for _ in range(3):
            piece = rem.astype(BF16)
            terms.append(piece)
            rem = rem - piece.astype(F32)
    pos = jnp.stack(terms, axis=-1)
    pos = jnp.concatenate([pos, jnp.zeros(pos.shape[:2] + (LANES - DA_POS_LANES,), BF16)], axis=-1)
    return dbias, pos


def _diff_attention(qkv, scal, subln_col, dbias, pos, B, S):
    T = B * S
    tq, tk = DA_TQ, DA_TK
    nq, nk = S // tq, S // tk
    kern = functools.partial(_da_kernel, nk=nk)
    return pl.pallas_call(
        kern,
        grid=(B, DA_HEADS, nq),
        in_specs=[
            pl.BlockSpec(memory_space=pltpu.SMEM),
            pl.BlockSpec((tq, LANES), lambda b, h, i: (b * nq + i, _DQ + h)),
            pl.BlockSpec((S, LANES), lambda b, h, i: (b, _DK + h)),
            pl.BlockSpec((S, LANES), lambda b, h, i: (b, _DV + h)),
            pl.BlockSpec((LANES, 1), lambda b, h, i: (0, 0)),
            pl.BlockSpec((1, tk, 2 * tq), lambda b, h, i: (h, 0, 0)),
            pl.BlockSpec((1, tk, LANES), lambda b, h, i: (h, 0, 0)),
        ],
        out_specs=pl.BlockSpec((tq, LANES), lambda b, h, i: (b * nq + i, h)),
        out_shape=jax.ShapeDtypeStruct((T, DA_WIDTH), BF16),
        scratch_shapes=[
            pltpu.VMEM((2 * LANES, 2 * tq), BF16),
            pltpu.VMEM((nk, DA_VROWS, tk), BF16),
            pltpu.VMEM((1, 2 * tq), F32),
            pltpu.VMEM((DA_VROWS, 2 * tq), F32),
            pltpu.VMEM((tk, 2 * tq), F32),
            pltpu.VMEM((tk, 2 * tq), F32),
        ],
        compiler_params=_cparams(("parallel", "parallel", "arbitrary")),
        name="diff_attn",
    )(scal, qkv, qkv, qkv, subln_col, dbias, pos)


NA_QROWS = 8
NA_KROWS = 16
NA_TQ = NA_QROWS * GRID_W
NA_TK = NA_KROWS * GRID_W


def _na_bias_tables(rpb, rows):
    cols = np.arange(GRID_W)
    cstart = np.clip(cols - NA_WIN_C // 2, 0, GRID_W - NA_WIN_C)
    kc = cols[None, :]
    col_ok = (kc >= cstart[:, None]) & (kc < cstart[:, None] + NA_WIN_C)
    dc = np.clip(kc - cols[:, None] + (NA_WIN_C - 1), 0, 2 * NA_WIN_C - 2)
    t = jnp.where(jnp.asarray(col_ok)[None, None], rpb[:, :, dc], NEG)
    t = jnp.concatenate([t, jnp.full((NA_HEADS, 1, GRID_W, GRID_W), NEG, F32)], axis=1)
    tables = []
    for r0, w0 in ((0, 0), (rows // 2 // NA_QROWS * NA_QROWS, None), (rows - NA_QROWS, rows - NA_KROWS)):
        if w0 is None:
            r0 = max(NA_QROWS, min(r0, rows - 2 * NA_QROWS))
            w0 = r0 - NA_WIN_R // 2
        r = r0 + np.arange(NA_QROWS)[:, None]
        kr = w0 + np.arange(NA_KROWS)[None, :]
        rs = np.clip(r - NA_WIN_R // 2, 0, rows - NA_WIN_R)
        ok = (kr >= rs) & (kr < rs + NA_WIN_R)
        dr = np.where(ok, kr - r + (NA_WIN_R - 1), 2 * NA_WIN_R - 1)
        tb = t[:, dr]
        tables.append(jnp.transpose(tb, (0, 1, 3, 2, 4)).reshape(NA_HEADS, NA_TQ, NA_TK))
    return jnp.stack(tables)


def _na_kernel(q_ref, k_ref, v_ref, b_ref, o_ref, *, rows):
    j = pl.program_id(2)
    w0 = jnp.clip(j * NA_QROWS - NA_WIN_R // 2, 0, rows - NA_KROWS)
    start = pl.multiple_of(w0 * GRID_W, GRID_W)
    kw = k_ref[pl.ds(start, NA_TK), :]
    vw = v_ref[pl.ds(start, NA_TK), :]
    q = q_ref[...]
    lane = lax.broadcasted_iota(jnp.int32, (NA_TQ, LANES), 1)
    zero = jnp.zeros_like(q)
    outs = []
    for hh in range(2):
        msk = (lane < HEAD_DIM) if hh == 0 else (lane >= HEAD_DIM)
        qm = jnp.where(msk, q, zero)
        s = lax.dot_general(qm, kw, (((1,), (1,)), ((), ())), preferred_element_type=F32)
        s = s + b_ref[0, hh]
        m = jnp.max(s, axis=-1, keepdims=True)
        p = jnp.exp(s - m)
        l = jnp.sum(p, axis=-1, keepdims=True)
        outs.append(jnp.dot(p.astype(BF16), vw, preferred_element_type=F32) / l)
    o_ref[...] = jnp.where(lane < HEAD_DIM, outs[0], outs[1]).astype(BF16)


def _neighborhood_attention(qkv, tables, B, S):
    T = B * S
    rows = S // GRID_W
    nblk = rows // NA_QROWS
    kern = functools.partial(_na_kernel, rows=rows)

    def btype(j):
        return jnp.where(j == 0, 0, jnp.where(j == nblk - 1, 2, 1))

    return pl.pallas_call(
        kern,
        grid=(NA_HEADS // 2, B, nblk),
        in_specs=[
            pl.BlockSpec((NA_TQ, LANES), lambda p, b, j: (b * nblk + j, _NQ + p)),
            pl.BlockSpec((S, LANES), lambda p, b, j: (b, _NK + p)),
            pl.BlockSpec((S, LANES), lambda p, b, j: (b, _NV + p)),
            pl.BlockSpec((1, 2, NA_TQ, NA_TK), lambda p, b, j: (btype(j), p, 0, 0)),
        ],
        out_specs=pl.BlockSpec((NA_TQ, LANES), lambda p, b, j: (b * nblk + j, p)),
        out_shape=jax.ShapeDtypeStruct((T, NA_WIDTH), BF16),
        compiler_params=_cparams(("parallel", "parallel", "parallel")),
        name="nbr_attn",
    )(qkv, qkv, qkv, tables)


def _mix_kernel(x_ref, oda_ref, ona_ref, wo_ref, g2_ref, wqt_ref, sk_ref, x1_ref, xn_ref, st_ref):
    x1 = (x_ref[...]
          + jnp.dot(oda_ref[...], wo_ref[0:DA_WIDTH, :], preferred_element_type=F32)
          + jnp.dot(ona_ref[...], wo_ref[DA_WIDTH:, :], preferred_element_type=F32))
    x1_ref[...] = x1
    ms = jnp.mean(x1 * x1, axis=-1, keepdims=True)
    xn = (x1 * lax.rsqrt(ms + RMS_EPS) * g2_ref[...]).astype(BF16)
    xn_ref[...] = xn
    qt = lax.dot_general(wqt_ref[...], xn, (((1,), (1,)), ((), ())), preferred_element_type=F32).astype(BF16)
    for hp in range(2 * PEER_HEADS):
        st_ref[hp * PEER_KEYS:(hp + 1) * PEER_KEYS, :] = jnp.dot(
            sk_ref[hp], qt[hp * LANES:(hp + 1) * LANES, :], preferred_element_type=F32)


def _mix(x2d, oda, ona, wo_bf, g2, wqt_bf, sk_bf, tm=512):
    T = x2d.shape[0]
    nsc = 2 * PEER_HEADS * PEER_KEYS
    return pl.pallas_call(
        _mix_kernel,
        grid=(T // tm,),
        in_specs=[
            pl.BlockSpec((tm, D_MODEL), lambda i: (i, 0)),
            pl.BlockSpec((tm, DA_WIDTH), lambda i: (i, 0)),
            pl.BlockSpec((tm, NA_WIDTH), lambda i: (i, 0)),
            pl.BlockSpec((D_MODEL, D_MODEL), lambda i: (0, 0)),
            pl.BlockSpec((1, D_MODEL), lambda i: (0, 0)),
            pl.BlockSpec((nsc, D_MODEL), lambda i: (0, 0)),
            pl.BlockSpec((2 * PEER_HEADS, PEER_KEYS, LANES), lambda i: (0, 0, 0)),
        ],
        out_specs=[
            pl.BlockSpec((tm, D_MODEL), lambda i: (i, 0)),
            pl.BlockSpec((tm, D_MODEL), lambda i: (i, 0)),
            pl.BlockSpec((nsc, tm), lambda i: (0, i)),
        ],
        out_shape=[
            jax.ShapeDtypeStruct((T, D_MODEL), F32),
            jax.ShapeDtypeStruct((T, D_MODEL), BF16),
            jax.ShapeDtypeStruct((nsc, T), F32),
        ],
        compiler_params=_cparams(("parallel",)),
        name="mix",
    )(x2d, oda, ona, wo_bf, g2, wqt_bf, sk_bf)


def _top16_rows(s, nrows):
    L = s.shape[1]
    rid = lax.broadcasted_iota(jnp.int32, (nrows, L), 0).astype(F32)
    slot = lax.broadcasted_iota(jnp.int32, (PEER_TOPK, L), 0)
    vals = jnp.zeros((PEER_TOPK, L), F32)
    idxs = jnp.zeros((PEER_TOPK, L), F32)
    for it in range(PEER_TOPK):
        m = jnp.max(s, axis=0, keepdims=True)
        ix = jnp.min(jnp.where(s == m, rid, float(nrows)), axis=0, keepdims=True)
        vals = jnp.where(slot == it, m, vals)
        idxs = jnp.where(slot == it, ix, idxs)
        s = jnp.where(rid == ix, -jnp.inf, s)
    return vals, idxs


def _topk_kernel(st_ref, hi_ref, lo_ref, g_ref):
    L = st_ref.shape[1]
    ncand = 72
    r = lax.broadcasted_iota(jnp.int32, (ncand, L), 0)
    pos = jnp.where(r < 16, r,
                    jnp.where(r < 48, (1 + ((r - 16) >> 3)) * PEER_TOPK + ((r - 16) & 7),
                              jnp.where(r < 64, (r - 48) * PEER_TOPK, (r - 64) * PEER_TOPK + 1))).astype(F32)
    dup = ((r >= 48) & (r < 53)) | ((r >= 64) & (r < 69))
    slot = lax.broadcasted_iota(jnp.int32, (PEER_TOPK, L), 0)
    slotf = slot.astype(F32)
    his, los, gates = [], [], []
    for h in range(PEER_HEADS):
        s1, i1 = _top16_rows(st_ref[(2 * h) * PEER_KEYS:(2 * h + 1) * PEER_KEYS, :], PEER_KEYS)
        s2, i2 = _top16_rows(st_ref[(2 * h + 1) * PEER_KEYS:(2 * h + 2) * PEER_KEYS, :], PEER_KEYS)
        cand = [s1[0:1, :] + s2]
        for a in range(1, 5):
            cand.append(s1[a:a + 1, :] + s2[0:8, :])
        cand.append(s1 + s2[0:1, :])
        cand.append(s1[0:8, :] + s2[1:2, :])
        cand = jnp.where(dup, -jnp.inf, jnp.concatenate(cand, axis=0))
        top = jnp.zeros((PEER_TOPK, L), F32)
        thi = jnp.zeros((PEER_TOPK, L), F32)
        tlo = jnp.zeros((PEER_TOPK, L), F32)
        for it in range(PEER_TOPK):
            m = jnp.max(cand, axis=0, keepdims=True)
            p = jnp.min(jnp.where(cand == m, pos, 1e9), axis=0, keepdims=True)
            a = jnp.floor(p * (1.0 / PEER_TOPK))
            b = p - a * PEER_TOPK
            ehi = jnp.max(jnp.where(slotf == a, i1, -1.0), axis=0, keepdims=True)
            elo = jnp.max(jnp.where(slotf == b, i2, -1.0), axis=0, keepdims=True)
            top = jnp.where(slot == it, m, top)
            thi = jnp.where(slot == it, ehi, thi)
            tlo = jnp.where(slot == it, elo, tlo)
            cand = jnp.where(pos == p, -jnp.inf, cand)
        e = jnp.exp(top - top[0:1, :])
        his.append(thi)
        los.append(tlo)
        gates.append(e / jnp.sum(e, axis=0, keepdims=True))
    hi_ref[...] = jnp.concatenate(his, axis=0).T
    lo_ref[...] = jnp.concatenate(los, axis=0).T
    g_ref[...] = jnp.concatenate(gates, axis=0).T


def _topk(st, L=128):
    nsc, T = st.shape
    nsel = PEER_HEADS * PEER_TOPK
    out = jax.ShapeDtypeStruct((T, nsel), F32)
    return pl.pallas_call(
        _topk_kernel,
        grid=(T // L,),
        in_specs=[pl.BlockSpec((nsc, L), lambda i: (0, i))],
        out_specs=[pl.BlockSpec((L, nsel), lambda i: (i, 0))] * 3,
        out_shape=[out, out, out],
        compiler_params=_cparams(("parallel",)),
        name="peer_topk",
    )(st)


GATE_GRP = 16
GATE_SLOTS = 2
GATE_PITCH = 136


def _gates_kernel(hi_ref, lo_ref, g_ref, o_ref, gs_ref, *, tg):
    nsel = PEER_HEADS * PEER_TOPK
    rid = lax.broadcasted_iota(jnp.int32, (PEER_KEYS, nsel), 0).astype(F32)

    def build(t0, slot):
        hi16 = hi_ref[pl.ds(t0, GATE_GRP), :]
        lo16 = lo_ref[pl.ds(t0, GATE_GRP), :]
        g16 = g_ref[pl.ds(t0, GATE_GRP), :]
        for tt in range(GATE_GRP):
            qt = jnp.where(rid == hi16[tt:tt + 1, :], 1.0, 0.0).astype(BF16)
            pt = jnp.where(rid == lo16[tt:tt + 1, :], g16[tt:tt + 1, :], 0.0).astype(BF16)
            r0 = (slot * GATE_GRP + tt) * GATE_PITCH
            gs_ref[r0:r0 + PEER_KEYS, :] = lax.dot_general(
                qt, pt, (((1,), (1,)), ((), ())), preferred_element_type=F32)

    def emit(t0, slot):
        base = slot * GATE_GRP * GATE_PITCH
        for i1 in range(PEER_KEYS):
            rows = gs_ref[pl.ds(base + i1, GATE_GRP, stride=GATE_PITCH), :]
            o_ref[i1, pl.ds(t0, GATE_GRP), :] = rows.astype(BF16)

    def groups(gi, carry):
        t0 = pl.multiple_of(gi * (GATE_SLOTS * GATE_GRP), GATE_SLOTS * GATE_GRP)
        for slot in range(GATE_SLOTS):
            build(t0 + slot * GATE_GRP, slot)
        for slot in range(GATE_SLOTS):
            emit(t0 + slot * GATE_GRP, slot)
        return carry

    lax.fori_loop(0, tg // (GATE_SLOTS * GATE_GRP), groups, 0)


def _gates(hi, lo, g, tg=256):
    T, nsel = hi.shape
    kern = functools.partial(_gates_kernel, tg=tg)
    return pl.pallas_call(
        kern,
        grid=(T // tg,),
        in_specs=[pl.BlockSpec((tg, nsel), lambda i: (i, 0))] * 3,
        out_specs=pl.BlockSpec((PEER_KEYS, tg, PEER_KEYS), lambda i: (0, i, 0)),
        out_shape=jax.ShapeDtypeStruct((PEER_KEYS, T, PEER_KEYS), BF16),
        scratch_shapes=[pltpu.VMEM((GATE_SLOTS * GATE_GRP * GATE_PITCH, LANES), F32)],
        compiler_params=_cparams(("parallel",)),
        name="peer_gates",
    )(hi, lo, g)


PEER_TM = 1024
PEER_EBLK = 1024


def _gelu_tanh(x):
    return 0.5 * x * (1.0 + jnp.tanh(math.sqrt(2.0 / math.pi) * (x + 0.044715 * (x * x * x))))


def _peer_kernel(x1_ref, xn_ref, ut_ref, v_ref, g3_ref, y_ref):
    j = pl.program_id(1)

    @pl.when(j == 0)
    def _residual():
        y_ref[...] = x1_ref[...]

    hmat = jnp.dot(xn_ref[...], ut_ref[...], preferred_element_type=F32)
    gates = jnp.concatenate([g3_ref[c] for c in range(PEER_EBLK // PEER_KEYS)], axis=1)
    w = (gates.astype(F32) * _gelu_tanh(hmat)).astype(BF16)
    y_ref[...] += jnp.dot(w, v_ref[...], preferred_element_type=F32)


def _peer(x1, xn, ut_bf, v_bf, g3):
    T = x1.shape[0]
    tm = PEER_TM
    return pl.pallas_call(
        _peer_kernel,
        grid=(T // tm, PEER_EXPERTS // PEER_EBLK),
        in_specs=[
            pl.BlockSpec((tm, D_MODEL), lambda i, j: (i, 0)),
            pl.BlockSpec((tm, D_MODEL), lambda i, j: (i, 0)),
            pl.BlockSpec((D_MODEL, PEER_EBLK), lambda i, j: (0, j)),
            pl.BlockSpec((PEER_EBLK, D_MODEL), lambda i, j: (j, 0)),
            pl.BlockSpec((PEER_EBLK // PEER_KEYS, tm, PEER_KEYS), lambda i, j: (j, i, 0)),
        ],
        out_specs=pl.BlockSpec((tm, D_MODEL), lambda i, j: (i, 0)),
        out_shape=jax.ShapeDtypeStruct((T, D_MODEL), F32),
        compiler_params=_cparams(("parallel", "arbitrary")),
        name="peer_mlp",
    )(x1, xn, ut_bf, v_bf, g3)


def _prepare_params(layer_idx, ln1_g, w_in, da_q_norm_g, da_k_norm_g, lam_q1, lam_k1, lam_q2, lam_k2,
                    da_subln_g, na_q_norm_g, na_k_norm_g, na_rpb, w_out, ln2_g,
                    peer_w_query, peer_sub_keys, peer_u, peer_v):
    scale = HEAD_DIM ** -0.5
    gn = jnp.stack([jnp.tile(da_q_norm_g, 8) * (scale * LOG2E), jnp.tile(da_k_norm_g, 8),
                    jnp.tile(na_q_norm_g, 8) * scale, jnp.tile(na_k_norm_g, 8)]).astype(F32)
    grp = np.arange(512) // HEAD_DIM
    gm = jnp.asarray((grp[:, None] == grp[None, :]).astype(np.float32) / HEAD_DIM, BF16)
    lambda_init = 0.8 - 0.6 * math.exp(-0.3 * layer_idx)
    lam = (jnp.exp(jnp.sum(lam_q1.astype(F32) * lam_k1.astype(F32)))
           - jnp.exp(jnp.sum(lam_q2.astype(F32) * lam_k2.astype(F32))) + lambda_init)
    slopes = jnp.asarray([2.0 ** (-8.0 * (h + 1) / DA_HEADS) for h in range(DA_HEADS)], F32)
    scal = jnp.concatenate([slopes, lam.reshape(1), jnp.full((1,), 1.0 - lambda_init, F32)])
    dbias, pos = _da_tables(slopes)
    return dict(
        g1=ln1_g.reshape(1, D_MODEL), w_in=w_in.astype(BF16), gn=gn, gm=gm, scal=scal,
        subln=da_subln_g.reshape(LANES, 1), dbias=dbias, pos=pos, rpb=na_rpb,
        w_out=w_out.astype(BF16), g2=ln2_g.reshape(1, D_MODEL), wqt=peer_w_query.T.astype(BF16),
        sk=peer_sub_keys.reshape(2 * PEER_HEADS, PEER_KEYS, LANES).astype(BF16),
        ut=peer_u.T.astype(BF16), v=peer_v.astype(BF16))


def _encoder_layer(x, p, tables):
    B, S, _ = x.shape
    x2d = x.reshape(B * S, D_MODEL)
    qkv = _inproj(x2d, p["g1"], p["w_in"], p["gn"], p["gm"])
    oda = _diff_attention(qkv, p["scal"], p["subln"], p["dbias"], p["pos"], B, S)
    ona = _neighborhood_attention(qkv, tables, B, S)
    x1, xn, st = _mix(x2d, oda, ona, p["w_out"], p["g2"], p["wqt"], p["sk"])
    hi, lo, g = _topk(st)
    g3 = _gates(hi, lo, g)
    y = _peer(x1, xn, p["ut"], p["v"], g3)
    return y.reshape(B, S, D_MODEL)


def kernel(x_prompt, x_sample, ln1_g, w_in, da_q_norm_g, da_k_norm_g, da_lambda_q1, da_lambda_k1,
           da_lambda_q2, da_lambda_k2, da_subln_g, na_q_norm_g, na_k_norm_g, na_rpb, w_out, ln2_g,
           peer_w_query, peer_sub_keys, peer_u, peer_v):
    hp, hs = x_prompt, x_sample
    for l in range(ln1_g.shape[0]):
        p = _prepare_params(l, ln1_g[l], w_in[l], da_q_norm_g[l], da_k_norm_g[l], da_lambda_q1[l],
                            da_lambda_k1[l], da_lambda_q2[l], da_lambda_k2[l], da_subln_g[l],
                            na_q_norm_g[l], na_k_norm_g[l], na_rpb[l], w_out[l], ln2_g[l],
                            peer_w_query[l], peer_sub_keys[l], peer_u[l], peer_v[l])
        outs = []
        for x in (hp, hs):
            tables = _na_bias_tables(p["rpb"], x.shape[1] // GRID_W)
            outs.append(_encoder_layer(x, p, tables))
        hp, hs = outs
    return (hp, hs)
```

```python
import functools
import math

import numpy as np
import jax
import jax.numpy as jnp
from jax import lax
from jax.experimental import pallas as pl
from jax.experimental.pallas import tpu as pltpu

F32 = jnp.float32
BF16 = jnp.bfloat16

D_MODEL = 1024
HEAD_DIM = 64
LANES = 128
MXU_N = 256
DA_HEADS = 4
DA_WIDTH = 512
NA_HEADS = 8
NA_WIDTH = 512
GRID_W = 64
NA_WIN_R = 8
NA_WIN_C = 16
IN_COLS = 3 * DA_WIDTH + 3 * NA_WIDTH
PEER_HEADS = 8
PEER_KEYS = 128
PEER_QDIM = 256
PEER_TOPK = 16
PEER_EXPERTS = PEER_KEYS * PEER_KEYS
RMS_EPS = 1e-6
NEG = -1e30
VMEM_LIMIT = 56 * 1024 * 1024

_DQ, _DK, _DV = 0, 4, 8
_NQ, _NK, _NV = 12, 16, 20


def _cparams(sem):
    return pltpu.CompilerParams(dimension_semantics=sem, vmem_limit_bytes=VMEM_LIMIT)


def _group_ms(x, gm):
    sq = x * x
    hi = sq.astype(BF16)
    lo = (sq - hi.astype(F32)).astype(BF16)
    return (jnp.dot(hi, gm, preferred_element_type=F32)
            + jnp.dot(lo, gm, preferred_element_type=F32))


def _inproj_kernel(x_ref, g1_ref, w_ref, gn_ref, gm_ref, o_ref):
    x = x_ref[...]
    ms = jnp.mean(x * x, axis=-1, keepdims=True)
    xn = (x * lax.rsqrt(ms + RMS_EPS) * g1_ref[...]).astype(BF16)
    proj = jnp.dot(xn, w_ref[...], preferred_element_type=F32)
    gm = gm_ref[...]
    for sec, row in ((0, 0), (1, 1), (3, 2), (4, 3)):
        xs = proj[:, sec * 512:(sec + 1) * 512]
        y = xs * lax.rsqrt(_group_ms(xs, gm) + RMS_EPS) * gn_ref[row:row + 1, :]
        o_ref[:, sec * 512:(sec + 1) * 512] = y.astype(BF16)
    for sec in (2, 5):
        o_ref[:, sec * 512:(sec + 1) * 512] = proj[:, sec * 512:(sec + 1) * 512].astype(BF16)


def _inproj(x2d, g1, w_in_bf, gn, gm, tm=512):
    T = x2d.shape[0]
    return pl.pallas_call(
        _inproj_kernel,
        grid=(T // tm,),
        in_specs=[
            pl.BlockSpec((tm, D_MODEL), lambda i: (i, 0)),
            pl.BlockSpec((1, D_MODEL), lambda i: (0, 0)),
            pl.BlockSpec((D_MODEL, IN_COLS), lambda i: (0, 0)),
            pl.BlockSpec((4, 512), lambda i: (0, 0)),
            pl.BlockSpec((512, 512), lambda i: (0, 0)),
        ],
        out_specs=pl.BlockSpec((tm, IN_COLS), lambda i: (i, 0)),
        out_shape=jax.ShapeDtypeStruct((T, IN_COLS), BF16),
        compiler_params=_cparams(("parallel",)),
        name="inproj",
    )(x2d, g1, w_in_bf, gn, gm)


DA_TQ = 512
DA_TK = 512
DA_POS_LANES = 6
DA_VROWS = 144
LOG2E = math.log2(math.e)


def _da_kernel(sc_ref, q_ref, k_ref, v_ref, g_ref, db_ref, pos_ref, o_ref,
               qst_ref, vt_ref, m_ref, acc_ref, sta_ref, stb_ref, *, nk):
    tq, tk = DA_TQ, DA_TK
    h = pl.program_id(1)
    qi = pl.program_id(2)
    slope = sc_ref[h]
    lam = sc_ref[4]
    post = sc_ref[5]

    @pl.when(qi == 0)
    def _transpose_values():
        ones_row = jnp.where(lax.broadcasted_iota(jnp.int32, (DA_VROWS - LANES, tk), 0) == 0, 1.0, 0.0)
        for c in range(nk):
            vt_ref[c, 0:LANES, :] = v_ref[c * tk:(c + 1) * tk, :].astype(F32).T.astype(BF16)
            vt_ref[c, LANES:DA_VROWS, :] = ones_row.astype(BF16)

    qT = q_ref[...].astype(F32).T
    row = lax.broadcasted_iota(jnp.int32, (LANES, tq), 0)
    qst_ref[0:LANES, 0:tq] = jnp.where(row < HEAD_DIM, qT, 0.0).astype(BF16)
    qst_ref[0:LANES, tq:2 * tq] = jnp.where(row >= HEAD_DIM, qT, 0.0).astype(BF16)
    row2 = lax.broadcasted_iota(jnp.int32, (LANES, 2 * tq), 0)
    qst_ref[LANES:2 * LANES, :] = jnp.where(row2 < DA_POS_LANES, 1.0, 0.0).astype(BF16)
    m_ref[...] = jnp.full((1, 2 * tq), NEG, F32)
    acc_ref[...] = jnp.zeros((DA_VROWS, 2 * tq), F32)

    pos = pos_ref[0].astype(F32)
    col = lax.broadcasted_iota(jnp.int32, (1, 2 * tq), 1)
    icol = jnp.where(col >= tq, col - tq, col).astype(F32)

    nch = 2 * tq // MXU_N

    def cols(n):
        return slice(n * MXU_N, (n + 1) * MXU_N)

    def side(blk):
        return jnp.where(qi > blk, 1.0, jnp.where(qi < blk, -1.0, 0.0))

    def key_operand(blk, sgn):
        k0 = pl.multiple_of(blk * tk, tk)
        return jnp.concatenate([k_ref[pl.ds(k0, tk), :], (pos * sgn).astype(BF16)], axis=1)

    def softmax_pv(n, src_ref, vtb, cq):
        sl = cols(n)
        st = src_ref[:, sl]
        c_n = cq[:, sl]
        m_old = m_ref[:, sl]
        m_new = jnp.maximum(m_old, jnp.max(st, axis=0, keepdims=True) + c_n)
        p = jnp.exp2(st - (m_new - c_n)).astype(BF16)
        alpha = jnp.exp2(m_old - m_new)
        acc_ref[:, sl] = alpha * acc_ref[:, sl] + jnp.dot(vtb, p, preferred_element_type=F32)
        m_ref[:, sl] = m_new

    def step(cur, nxt, src_ref, dst_ref):
        ka = key_operand(nxt, side(nxt))
        vtb = vt_ref[cur]
        cq = (-side(cur) * slope * LOG2E) * ((qi * tq - cur * tk).astype(F32) + icol)

        def scores(n):
            dst_ref[:, cols(n)] = jnp.dot(ka, qst_ref[:, cols(n)], preferred_element_type=F32)

        scores(0)
        scores(1)
        for n in range(nch):
            softmax_pv(n, src_ref, vtb, cq)
            if n + 2 < nch:
                scores(n + 2)

    def other(x):
        return x + (x >= qi).astype(jnp.int32)

    ka0 = key_operand(qi, 0.0)
    for n in range(nch):
        sta_ref[:, cols(n)] = (jnp.dot(ka0, qst_ref[:, cols(n)], preferred_element_type=F32)
                               + db_ref[0, :, cols(n)])

    unroll = 8 if nk % 8 == 0 else (4 if nk % 4 == 0 else 2)

    def trip(ii, carry):
        cur = jnp.where(ii == 0, qi, other(unroll * ii - 1))
        for s in range(unroll):
            nxt = other(jnp.minimum(unroll * ii + s, nk - 2))
            src_ref, dst_ref = (sta_ref, stb_ref) if s % 2 == 0 else (stb_ref, sta_ref)
            step(cur, nxt, src_ref, dst_ref)
            cur = nxt
        return carry

    lax.fori_loop(0, nk // unroll, trip, 0)

    on = acc_ref[0:LANES, :] / acc_ref[LANES:LANES + 1, :]
    oT = on[:, 0:tq] - lam * on[:, tq:2 * tq]
    ms = jnp.mean(oT * oT, axis=0, keepdims=True)
    oT = oT * lax.rsqrt(ms + RMS_EPS) * g_ref[...] * post
    o_ref[...] = oT.T.astype(BF16)


def _da_tables(slopes):
    jj = np.arange(DA_TK)[:, None]
    ii = np.concatenate([np.arange(DA_TQ), np.arange(DA_TQ)])[None, :]
    dist = jnp.asarray(np.abs(ii - jj).astype(np.float32))
    dbias = -(slopes * LOG2E)[:, None, None] * dist[None]
    j = np.arange(DA_TK)
    terms = []
    for part in ((j >> 5) * 32.0, (j & 31) * 1.0):
        rem = (slopes * LOG2E)[:, None] * jnp.asarray(part.astype(np.float32))[None, :]
        for _ in range(3):
            piece = rem.astype(BF16)
            terms.append(piece)
            rem = rem - piece.astype(F32)
    pos = jnp.stack(terms, axis=-1)
    pos = jnp.concatenate([pos, jnp.zeros(pos.shape[:2] + (LANES - DA_POS_LANES,), BF16)], axis=-1)
    return dbias, pos


def _diff_attention(qkv, scal, subln_col, dbias, pos, B, S):
    T = B * S
    tq, tk = DA_TQ, DA_TK
    nq, nk = S // tq, S // tk
    kern = functools.partial(_da_kernel, nk=nk)
    return pl.pallas_call(
        kern,
        grid=(B, DA_HEADS, nq),
        in_specs=[
            pl.BlockSpec(memory_space=pltpu.SMEM),
            pl.BlockSpec((tq, LANES), lambda b, h, i: (b * nq + i, _DQ + h)),
            pl.BlockSpec((S, LANES), lambda b, h, i: (b, _DK + h)),
            pl.BlockSpec((S, LANES), lambda b, h, i: (b, _DV + h)),
            pl.BlockSpec((LANES, 1), lambda b, h, i: (0, 0)),
            pl.BlockSpec((1, tk, 2 * tq), lambda b, h, i: (h, 0, 0)),
            pl.BlockSpec((1, tk, LANES), lambda b, h, i: (h, 0, 0)),
        ],
        out_specs=pl.BlockSpec((tq, LANES), lambda b, h, i: (b * nq + i, h)),
        out_shape=jax.ShapeDtypeStruct((T, DA_WIDTH), BF16),
        scratch_shapes=[
            pltpu.VMEM((2 * LANES, 2 * tq), BF16),
            pltpu.VMEM((nk, DA_VROWS, tk), BF16),
            pltpu.VMEM((1, 2 * tq), F32),
            pltpu.VMEM((DA_VROWS, 2 * tq), F32),
            pltpu.VMEM((tk, 2 * tq), F32),
            pltpu.VMEM((tk, 2 * tq), F32),
        ],
        compiler_params=_cparams(("parallel", "parallel", "arbitrary")),
        name="diff_attn",
    )(scal, qkv, qkv, qkv, subln_col, dbias, pos)


NA_QROWS = 8
NA_KROWS = 16
NA_TQ = NA_QROWS * GRID_W
NA_TK = NA_KROWS * GRID_W


def _na_bias_tables(rpb, rows):
    cols = np.arange(GRID_W)
    cstart = np.clip(cols - NA_WIN_C // 2, 0, GRID_W - NA_WIN_C)
    kc = cols[None, :]
    col_ok = (kc >= cstart[:, None]) & (kc < cstart[:, None] + NA_WIN_C)
    dc = np.clip(kc - cols[:, None] + (NA_WIN_C - 1), 0, 2 * NA_WIN_C - 2)
    t = jnp.where(jnp.asarray(col_ok)[None, None], rpb[:, :, dc], NEG)
    t = jnp.concatenate([t, jnp.full((NA_HEADS, 1, GRID_W, GRID_W), NEG, F32)], axis=1)
    tables = []
    for r0, w0 in ((0, 0), (rows // 2 // NA_QROWS * NA_QROWS, None), (rows - NA_QROWS, rows - NA_KROWS)):
        if w0 is None:
            r0 = max(NA_QROWS, min(r0, rows - 2 * NA_QROWS))
            w0 = r0 - NA_WIN_R // 2
        r = r0 + np.arange(NA_QROWS)[:, None]
        kr = w0 + np.arange(NA_KROWS)[None, :]
        rs = np.clip(r - NA_WIN_R // 2, 0, rows - NA_WIN_R)
        ok = (kr >= rs) & (kr < rs + NA_WIN_R)
        dr = np.where(ok, kr - r + (NA_WIN_R - 1), 2 * NA_WIN_R - 1)
        tb = t[:, dr]
        tables.append(jnp.transpose(tb, (0, 1, 3, 2, 4)).reshape(NA_HEADS, NA_TQ, NA_TK))
    return jnp.stack(tables)


def _na_kernel(q_ref, k_ref, v_ref, b_ref, o_ref, *, rows):
    j = pl.program_id(2)
    w0 = jnp.clip(j * NA_QROWS - NA_WIN_R // 2, 0, rows - NA_KROWS)
    start = pl.multiple_of(w0 * GRID_W, GRID_W)
    kw = k_ref[pl.ds(start, NA_TK), :]
    vw = v_ref[pl.ds(start, NA_TK), :]
    q = q_ref[...]
    lane = lax.broadcasted_iota(jnp.int32, (NA_TQ, LANES), 1)
    zero = jnp.zeros_like(q)
    outs = []
    for hh in range(2):
        msk = (lane < HEAD_DIM) if hh == 0 else (lane >= HEAD_DIM)
        qm = jnp.where(msk, q, zero)
        s = lax.dot_general(qm, kw, (((1,), (1,)), ((), ())), preferred_element_type=F32)
        s = s + b_ref[0, hh]
        m = jnp.max(s, axis=-1, keepdims=True)
        p = jnp.exp(s - m)
        l = jnp.sum(p, axis=-1, keepdims=True)
        outs.append(jnp.dot(p.astype(BF16), vw, preferred_element_type=F32) / l)
    o_ref[...] = jnp.where(lane < HEAD_DIM, outs[0], outs[1]).astype(BF16)


def _neighborhood_attention(qkv, tables, B, S):
    T = B * S
    rows = S // GRID_W
    nblk = rows // NA_QROWS
    kern = functools.partial(_na_kernel, rows=rows)

    def btype(j):
        return jnp.where(j == 0, 0, jnp.where(j == nblk - 1, 2, 1))

    return pl.pallas_call(
        kern,
        grid=(NA_HEADS // 2, B, nblk),
        in_specs=[
            pl.BlockSpec((NA_TQ, LANES), lambda p, b, j: (b * nblk + j, _NQ + p)),
            pl.BlockSpec((S, LANES), lambda p, b, j: (b, _NK + p)),
            pl.BlockSpec((S, LANES), lambda p, b, j: (b, _NV + p)),
            pl.BlockSpec((1, 2, NA_TQ, NA_TK), lambda p, b, j: (btype(j), p, 0, 0)),
        ],
        out_specs=pl.BlockSpec((NA_TQ, LANES), lambda p, b, j: (b * nblk + j, p)),
        out_shape=jax.ShapeDtypeStruct((T, NA_WIDTH), BF16),
        compiler_params=_cparams(("parallel", "parallel", "parallel")),
        name="nbr_attn",
    )(qkv, qkv, qkv, tables)


def _mix_kernel(x_ref, oda_ref, ona_ref, wo_ref, g2_ref, wqt_ref, sk_ref, x1_ref, xn_ref, st_ref):
    x1 = (x_ref[...]
          + jnp.dot(oda_ref[...], wo_ref[0:DA_WIDTH, :], preferred_element_type=F32)
          + jnp.dot(ona_ref[...], wo_ref[DA_WIDTH:, :], preferred_element_type=F32))
    x1_ref[...] = x1
    ms = jnp.mean(x1 * x1, axis=-1, keepdims=True)
    xn = (x1 * lax.rsqrt(ms + RMS_EPS) * g2_ref[...]).astype(BF16)
    xn_ref[...] = xn
    qt = lax.dot_general(wqt_ref[...], xn, (((1,), (1,)), ((), ())), preferred_element_type=F32).astype(BF16)
    for hp in range(2 * PEER_HEADS):
        st_ref[hp * PEER_KEYS:(hp + 1) * PEER_KEYS, :] = jnp.dot(
            sk_ref[hp], qt[hp * LANES:(hp + 1) * LANES, :], preferred_element_type=F32)


def _mix(x2d, oda, ona, wo_bf, g2, wqt_bf, sk_bf, tm=512):
    T = x2d.shape[0]
    nsc = 2 * PEER_HEADS * PEER_KEYS
    return pl.pallas_call(
        _mix_kernel,
        grid=(T // tm,),
        in_specs=[
            pl.BlockSpec((tm, D_MODEL), lambda i: (i, 0)),
            pl.BlockSpec((tm, DA_WIDTH), lambda i: (i, 0)),
            pl.BlockSpec((tm, NA_WIDTH), lambda i: (i, 0)),
            pl.BlockSpec((D_MODEL, D_MODEL), lambda i: (0, 0)),
            pl.BlockSpec((1, D_MODEL), lambda i: (0, 0)),
            pl.BlockSpec((nsc, D_MODEL), lambda i: (0, 0)),
            pl.BlockSpec((2 * PEER_HEADS, PEER_KEYS, LANES), lambda i: (0, 0, 0)),
        ],
        out_specs=[
            pl.BlockSpec((tm, D_MODEL), lambda i: (i, 0)),
            pl.BlockSpec((tm, D_MODEL), lambda i: (i, 0)),
            pl.BlockSpec((nsc, tm), lambda i: (0, i)),
        ],
        out_shape=[
            jax.ShapeDtypeStruct((T, D_MODEL), F32),
            jax.ShapeDtypeStruct((T, D_MODEL), BF16),
            jax.ShapeDtypeStruct((nsc, T), F32),
        ],
        compiler_params=_cparams(("parallel",)),
        name="mix",
    )(x2d, oda, ona, wo_bf, g2, wqt_bf, sk_bf)


def _top16_rows(s, nrows):
    L = s.shape[1]
    rid = lax.broadcasted_iota(jnp.int32, (nrows, L), 0).astype(F32)
    slot = lax.broadcasted_iota(jnp.int32, (PEER_TOPK, L), 0)
    vals = jnp.zeros((PEER_TOPK, L), F32)
    idxs = jnp.zeros((PEER_TOPK, L), F32)
    for it in range(PEER_TOPK):
        m = jnp.max(s, axis=0, keepdims=True)
        ix = jnp.min(jnp.where(s == m, rid, float(nrows)), axis=0, keepdims=True)
        vals = jnp.where(slot == it, m, vals)
        idxs = jnp.where(slot == it, ix, idxs)
        s = jnp.where(rid == ix, -jnp.inf, s)
    return vals, idxs


def _topk_kernel(st_ref, hi_ref, lo_ref, g_ref):
    L = st_ref.shape[1]
    ncand = 72
    r = lax.broadcasted_iota(jnp.int32, (ncand, L), 0)
    pos = jnp.where(r < 16, r,
                    jnp.where(r < 48, (1 + ((r - 16) >> 3)) * PEER_TOPK + ((r - 16) & 7),
                              jnp.where(r < 64, (r - 48) * PEER_TOPK, (r - 64) * PEER_TOPK + 1))).astype(F32)
    dup = ((r >= 48) & (r < 53)) | ((r >= 64) & (r < 69))
    slot = lax.broadcasted_iota(jnp.int32, (PEER_TOPK, L), 0)
    slotf = slot.astype(F32)
    his, los, gates = [], [], []
    for h in range(PEER_HEADS):
        s1, i1 = _top16_rows(st_ref[(2 * h) * PEER_KEYS:(2 * h + 1) * PEER_KEYS, :], PEER_KEYS)
        s2, i2 = _top16_rows(st_ref[(2 * h + 1) * PEER_KEYS:(2 * h + 2) * PEER_KEYS, :], PEER_KEYS)
        cand = [s1[0:1, :] + s2]
        for a in range(1, 5):
            cand.append(s1[a:a + 1, :] + s2[0:8, :])
        cand.append(s1 + s2[0:1, :])
        cand.append(s1[0:8, :] + s2[1:2, :])
        cand = jnp.where(dup, -jnp.inf, jnp.concatenate(cand, axis=0))
        top = jnp.zeros((PEER_TOPK, L), F32)
        thi = jnp.zeros((PEER_TOPK, L), F32)
        tlo = jnp.zeros((PEER_TOPK, L), F32)
        for it in range(PEER_TOPK):
            m = jnp.max(cand, axis=0, keepdims=True)
            p = jnp.min(jnp.where(cand == m, pos, 1e9), axis=0, keepdims=True)
            a = jnp.floor(p * (1.0 / PEER_TOPK))
            b = p - a * PEER_TOPK
            ehi = jnp.max(jnp.where(slotf == a, i1, -1.0), axis=0, keepdims=True)
            elo = jnp.max(jnp.where(slotf == b, i2, -1.0), axis=0, keepdims=True)
            top = jnp.where(slot == it, m, top)
            thi = jnp.where(slot == it, ehi, thi)
            tlo = jnp.where(slot == it, elo, tlo)
            cand = jnp.where(pos == p, -jnp.inf, cand)
        e = jnp.exp(top - top[0:1, :])
        his.append(thi)
        los.append(tlo)
        gates.append(e / jnp.sum(e, axis=0, keepdims=True))
    hi_ref[...] = jnp.concatenate(his, axis=0).T
    lo_ref[...] = jnp.concatenate(los, axis=0).T
    g_ref[...] = jnp.concatenate(gates, axis=0).T


def _topk(st, L=256):
    nsc, T = st.shape
    nsel = PEER_HEADS * PEER_TOPK
    out = jax.ShapeDtypeStruct((T, nsel), F32)
    return pl.pallas_call(
        _topk_kernel,
        grid=(T // L,),
        in_specs=[pl.BlockSpec((nsc, L), lambda i: (0, i))],
        out_specs=[pl.BlockSpec((L, nsel), lambda i: (i, 0))] * 3,
        out_shape=[out, out, out],
        compiler_params=_cparams(("parallel",)),
        name="peer_topk",
    )(st)


GATE_GRP = 128
GATE_ROWS = 8


def _gates_kernel(hi_ref, lo_ref, g_ref, o_ref, *, tg):
    nsel = PEER_HEADS * PEER_TOPK
    rid_b = lax.broadcasted_iota(jnp.int32, (PEER_KEYS, nsel), 0).astype(F32).astype(BF16)
    one_b = jnp.ones((PEER_KEYS, nsel), BF16)
    zero_b = jnp.zeros((PEER_KEYS, nsel), BF16)

    def group(gi, carry):
        t0 = pl.multiple_of(gi * GATE_GRP, GATE_GRP)
        hi16 = hi_ref[pl.ds(t0, GATE_GRP), :]
        lo16 = lo_ref[pl.ds(t0, GATE_GRP), :]
        g16 = g_ref[pl.ds(t0, GATE_GRP), :]
        for tt in range(GATE_GRP):
            hi_b = jnp.broadcast_to(hi16[tt:tt + 1, :].astype(BF16), (PEER_KEYS, nsel))
            lo_b = jnp.broadcast_to(lo16[tt:tt + 1, :].astype(BF16), (PEER_KEYS, nsel))
            g_b = jnp.broadcast_to(g16[tt:tt + 1, :].astype(BF16), (PEER_KEYS, nsel))
            qt = jnp.where(rid_b == hi_b, one_b, zero_b)
            pt = jnp.where(rid_b == lo_b, g_b, zero_b)
            gt = lax.dot_general(qt, pt, (((1,), (1,)), ((), ())), preferred_element_type=F32)
            for jb in range(PEER_EXPERTS // PEER_EBLK):
                o_ref[jb, t0 + tt] = gt[jb * GATE_ROWS:(jb + 1) * GATE_ROWS, :]
        return carry

    lax.fori_loop(0, tg // GATE_GRP, group, 0)


def _gates(hi, lo, g, tg=128):
    T, nsel = hi.shape
    kern = functools.partial(_gates_kernel, tg=tg)
    return pl.pallas_call(
        kern,
        grid=(T // tg,),
        in_specs=[pl.BlockSpec((tg, nsel), lambda i: (i, 0))] * 3,
        out_specs=pl.BlockSpec((PEER_EXPERTS // PEER_EBLK, tg, GATE_ROWS, PEER_KEYS), lambda i: (0, i, 0, 0)),
        out_shape=jax.ShapeDtypeStruct((PEER_EXPERTS // PEER_EBLK, T, GATE_ROWS, PEER_KEYS), F32),
        compiler_params=_cparams(("parallel",)),
        name="peer_gates",
    )(hi, lo, g)


PEER_TM = 1024
PEER_EBLK = 1024
assert GATE_ROWS * PEER_KEYS == PEER_EBLK


def _gelu_tanh(x):
    return 0.5 * x * (1.0 + jnp.tanh(math.sqrt(2.0 / math.pi) * (x + 0.044715 * (x * x * x))))


def _peer_kernel(x1_ref, xn_ref, ut_ref, v_ref, g3_ref, y_ref):
    j = pl.program_id(1)

    @pl.when(j == 0)
    def _residual():
        y_ref[...] = x1_ref[...]

    hmat = jnp.dot(xn_ref[...], ut_ref[...], preferred_element_type=F32)
    tm = xn_ref.shape[0]
    gates = jnp.concatenate([g3_ref[pl.ds(c, tm, stride=GATE_ROWS), :] for c in range(GATE_ROWS)], axis=1)
    w = (gates * _gelu_tanh(hmat)).astype(BF16)
    y_ref[...] += jnp.dot(w, v_ref[...], preferred_element_type=F32)


def _peer(x1, xn, ut_bf, v_bf, g3):
    T = x1.shape[0]
    tm = PEER_TM
    return pl.pallas_call(
        _peer_kernel,
        grid=(T // tm, PEER_EXPERTS // PEER_EBLK),
        in_specs=[
            pl.BlockSpec((tm, D_MODEL), lambda i, j: (i, 0)),
            pl.BlockSpec((tm, D_MODEL), lambda i, j: (i, 0)),
            pl.BlockSpec((D_MODEL, PEER_EBLK), lambda i, j: (0, j)),
            pl.BlockSpec((PEER_EBLK, D_MODEL), lambda i, j: (j, 0)),
            pl.BlockSpec((tm * GATE_ROWS, PEER_KEYS), lambda i, j: (j * (T // tm) + i, 0)),
        ],
        out_specs=pl.BlockSpec((tm, D_MODEL), lambda i, j: (i, 0)),
        out_shape=jax.ShapeDtypeStruct((T, D_MODEL), F32),
        compiler_params=_cparams(("parallel", "arbitrary")),
        name="peer_mlp",
    )(x1, xn, ut_bf, v_bf, g3.reshape(-1, PEER_KEYS))


def _prepare_params(layer_idx, ln1_g, w_in, da_q_norm_g, da_k_norm_g, lam_q1, lam_k1, lam_q2, lam_k2,
                    da_subln_g, na_q_norm_g, na_k_norm_g, na_rpb, w_out, ln2_g,
                    peer_w_query, peer_sub_keys, peer_u, peer_v):
    scale = HEAD_DIM ** -0.5
    gn = jnp.stack([jnp.tile(da_q_norm_g, 8) * (scale * LOG2E), jnp.tile(da_k_norm_g, 8),
                    jnp.tile(na_q_norm_g, 8) * scale, jnp.tile(na_k_norm_g, 8)]).astype(F32)
    grp = np.arange(512) // HEAD_DIM
    gm = jnp.asarray((grp[:, None] == grp[None, :]).astype(np.float32) / HEAD_DIM, BF16)
    lambda_init = 0.8 - 0.6 * math.exp(-0.3 * layer_idx)
    lam = (jnp.exp(jnp.sum(lam_q1.astype(F32) * lam_k1.astype(F32)))
           - jnp.exp(jnp.sum(lam_q2.astype(F32) * lam_k2.astype(F32))) + lambda_init)
    slopes = jnp.asarray([2.0 ** (-8.0 * (h + 1) / DA_HEADS) for h in range(DA_HEADS)], F32)
    scal = jnp.concatenate([slopes, lam.reshape(1), jnp.full((1,), 1.0 - lambda_init, F32)])
    dbias, pos = _da_tables(slopes)
    return dict(
        g1=ln1_g.reshape(1, D_MODEL), w_in=w_in.astype(BF16), gn=gn, gm=gm, scal=scal,
        subln=da_subln_g.reshape(LANES, 1), dbias=dbias, pos=pos, rpb=na_rpb,
        w_out=w_out.astype(BF16), g2=ln2_g.reshape(1, D_MODEL), wqt=peer_w_query.T.astype(BF16),
        sk=peer_sub_keys.reshape(2 * PEER_HEADS, PEER_KEYS, LANES).astype(BF16),
        ut=peer_u.T.astype(BF16), v=peer_v.astype(BF16))


def _encoder_layer(x, p, tables):
    B, S, _ = x.shape
    x2d = x.reshape(B * S, D_MODEL)
    qkv = _inproj(x2d, p["g1"], p["w_in"], p["gn"], p["gm"])
    oda = _diff_attention(qkv, p["scal"], p["subln"], p["dbias"], p["pos"], B, S)
    ona = _neighborhood_attention(qkv, tables, B, S)
    x1, xn, st = _mix(x2d, oda, ona, p["w_out"], p["g2"], p["wqt"], p["sk"])
    hi, lo, g = _topk(st)
    g3 = _gates(hi, lo, g)
    y = _peer(x1, xn, p["ut"], p["v"], g3)
    return y.reshape(B, S, D_MODEL)


def kernel(x_prompt, x_sample, ln1_g, w_in, da_q_norm_g, da_k_norm_g, da_lambda_q1, da_lambda_k1,
           da_lambda_q2, da_lambda_k2, da_subln_g, na_q_norm_g, na_k_norm_g, na_rpb, w_out, ln2_g,
           peer_w_query, peer_sub_keys, peer_u, peer_v):
    hp, hs = x_prompt, x_sample
    for l in range(ln1_g.shape[0]):
        p = _prepare_params(l, ln1_g[l], w_in[l], da_q_norm_g[l], da_k_norm_g[l], da_lambda_q1[l],
                            da_lambda_k1[l], da_lambda_q2[l], da_lambda_k2[l], da_subln_g[l],
                            na_q_norm_g[l], na_k_norm_g[l], na_rpb[l], w_out[l], ln2_g[l],
                            peer_w_query[l], peer_sub_keys[l], peer_u[l], peer_v[l])
        outs = []
        for x in (hp, hs):
            tables = _na_bias_tables(p["rpb"], x.shape[1] // GRID_W)
            outs.append(_encoder_layer(x, p, tables))
        hp, hs = outs
    return (hp, hs)
```

```python
import functools
import math

import numpy as np
import jax
import jax.numpy as jnp
from jax import lax
from jax.experimental import pallas as pl
from jax.experimental.pallas import tpu as pltpu

F32 = jnp.float32
BF16 = jnp.bfloat16

D_MODEL = 1024
HEAD_DIM = 64
LANES = 128
MXU_N = 256
DA_HEADS = 4
DA_WIDTH = 512
NA_HEADS = 8
NA_WIDTH = 512
GRID_W = 64
NA_WIN_R = 8
NA_WIN_C = 16
IN_COLS = 3 * DA_WIDTH + 3 * NA_WIDTH
PEER_HEADS = 8
PEER_KEYS = 128
PEER_QDIM = 256
PEER_TOPK = 16
PEER_EXPERTS = PEER_KEYS * PEER_KEYS
RMS_EPS = 1e-6
NEG = -1e30
VMEM_LIMIT = 56 * 1024 * 1024

_DQ, _DK, _DV = 0, 4, 8
_NQ, _NK, _NV = 12, 16, 20


def _cparams(sem):
    return pltpu.CompilerParams(dimension_semantics=sem, vmem_limit_bytes=VMEM_LIMIT)


def _group_ms(x, gm):
    sq = x * x
    hi = sq.astype(BF16)
    lo = (sq - hi.astype(F32)).astype(BF16)
    return (jnp.dot(hi, gm, preferred_element_type=F32)
            + jnp.dot(lo, gm, preferred_element_type=F32))


def _inproj_kernel(x_ref, g1_ref, w_ref, gn_ref, gm_ref, o_ref):
    x = x_ref[...]
    ms = jnp.mean(x * x, axis=-1, keepdims=True)
    xn = (x * lax.rsqrt(ms + RMS_EPS) * g1_ref[...]).astype(BF16)
    proj = jnp.dot(xn, w_ref[...], preferred_element_type=F32)
    gm = gm_ref[...]
    for sec, row in ((0, 0), (1, 1), (3, 2), (4, 3)):
        xs = proj[:, sec * 512:(sec + 1) * 512]
        y = xs * lax.rsqrt(_group_ms(xs, gm) + RMS_EPS) * gn_ref[row:row + 1, :]
        o_ref[:, sec * 512:(sec + 1) * 512] = y.astype(BF16)
    for sec in (2, 5):
        o_ref[:, sec * 512:(sec + 1) * 512] = proj[:, sec * 512:(sec + 1) * 512].astype(BF16)


def _inproj(x2d, g1, w_in_bf, gn, gm, tm=512):
    T = x2d.shape[0]
    return pl.pallas_call(
        _inproj_kernel,
        grid=(T // tm,),
        in_specs=[
            pl.BlockSpec((tm, D_MODEL), lambda i: (i, 0)),
            pl.BlockSpec((1, D_MODEL), lambda i: (0, 0)),
            pl.BlockSpec((D_MODEL, IN_COLS), lambda i: (0, 0)),
            pl.BlockSpec((4, 512), lambda i: (0, 0)),
            pl.BlockSpec((512, 512), lambda i: (0, 0)),
        ],
        out_specs=pl.BlockSpec((tm, IN_COLS), lambda i: (i, 0)),
        out_shape=jax.ShapeDtypeStruct((T, IN_COLS), BF16),
        compiler_params=_cparams(("parallel",)),
        name="inproj",
    )(x2d, g1, w_in_bf, gn, gm)


DA_TQ = 512
DA_TK = 512
DA_POS_LANES = 6
DA_VROWS = 144
LOG2E = math.log2(math.e)


def _da_kernel(sc_ref, q_ref, k_ref, v_ref, g_ref, db_ref, pos_ref, o_ref,
               qst_ref, vt_ref, m_ref, acc_ref, sta_ref, stb_ref, *, nk):
    tq, tk = DA_TQ, DA_TK
    h = pl.program_id(1)
    qi = pl.program_id(2)
    slope = sc_ref[h]
    lam = sc_ref[4]
    post = sc_ref[5]

    @pl.when(qi == 0)
    def _transpose_values():
        ones_row = jnp.where(lax.broadcasted_iota(jnp.int32, (DA_VROWS - LANES, tk), 0) == 0, 1.0, 0.0)
        for c in range(nk):
            vt_ref[c, 0:LANES, :] = v_ref[c * tk:(c + 1) * tk, :].astype(F32).T.astype(BF16)
            vt_ref[c, LANES:DA_VROWS, :] = ones_row.astype(BF16)

    qT = q_ref[...].astype(F32).T
    row = lax.broadcasted_iota(jnp.int32, (LANES, tq), 0)
    qst_ref[0:LANES, 0:tq] = jnp.where(row < HEAD_DIM, qT, 0.0).astype(BF16)
    qst_ref[0:LANES, tq:2 * tq] = jnp.where(row >= HEAD_DIM, qT, 0.0).astype(BF16)
    row2 = lax.broadcasted_iota(jnp.int32, (LANES, 2 * tq), 0)
    qst_ref[LANES:2 * LANES, :] = jnp.where(row2 < DA_POS_LANES, 1.0, 0.0).astype(BF16)
    m_ref[...] = jnp.full((1, 2 * tq), NEG, F32)
    acc_ref[...] = jnp.zeros((DA_VROWS, 2 * tq), F32)

    pos = pos_ref[0].astype(F32)
    col = lax.broadcasted_iota(jnp.int32, (1, 2 * tq), 1)
    icol = jnp.where(col >= tq, col - tq, col).astype(F32)

    nch = 2 * tq // MXU_N

    def cols(n):
        return slice(n * MXU_N, (n + 1) * MXU_N)

    def side(blk):
        return jnp.where(qi > blk, 1.0, jnp.where(qi < blk, -1.0, 0.0))

    def key_operand(blk, sgn):
        k0 = pl.multiple_of(blk * tk, tk)
        return jnp.concatenate([k_ref[pl.ds(k0, tk), :], (pos * sgn).astype(BF16)], axis=1)

    def softmax_pv(n, src_ref, vtb, cq):
        sl = cols(n)
        st = src_ref[:, sl]
        c_n = cq[:, sl]
        m_old = m_ref[:, sl]
        m_new = jnp.maximum(m_old, jnp.max(st, axis=0, keepdims=True) + c_n)
        p = jnp.exp2(st - (m_new - c_n)).astype(BF16)
        alpha = jnp.exp2(m_old - m_new)
        acc_ref[:, sl] = alpha * acc_ref[:, sl] + jnp.dot(vtb, p, preferred_element_type=F32)
        m_ref[:, sl] = m_new

    def step(cur, nxt, src_ref, dst_ref):
        ka = key_operand(nxt, side(nxt))
        vtb = vt_ref[cur]
        cq = (-side(cur) * slope * LOG2E) * ((qi * tq - cur * tk).astype(F32) + icol)

        def scores(n):
            dst_ref[:, cols(n)] = jnp.dot(ka, qst_ref[:, cols(n)], preferred_element_type=F32)

        scores(0)
        scores(1)
        for n in range(nch):
            softmax_pv(n, src_ref, vtb, cq)
            if n + 2 < nch:
                scores(n + 2)

    def other(x):
        return x + (x >= qi).astype(jnp.int32)

    ka0 = key_operand(qi, 0.0)
    for n in range(nch):
        sta_ref[:, cols(n)] = (jnp.dot(ka0, qst_ref[:, cols(n)], preferred_element_type=F32)
                               + db_ref[0, :, cols(n)])

    unroll = nk if nk <= 16 else (8 if nk % 8 == 0 else 2)

    def trip(ii, carry):
        cur = jnp.where(ii == 0, qi, other(unroll * ii - 1))
        for s in range(unroll):
            nxt = other(jnp.minimum(unroll * ii + s, nk - 2))
            src_ref, dst_ref = (sta_ref, stb_ref) if s % 2 == 0 else (stb_ref, sta_ref)
            step(cur, nxt, src_ref, dst_ref)
            cur = nxt
        return carry

    lax.fori_loop(0, nk // unroll, trip, 0)

    on = acc_ref[0:LANES, :] / acc_ref[LANES:LANES + 1, :]
    oT = on[:, 0:tq] - lam * on[:, tq:2 * tq]
    ms = jnp.mean(oT * oT, axis=0, keepdims=True)
    oT = oT * lax.rsqrt(ms + RMS_EPS) * g_ref[...] * post
    o_ref[...] = oT.T.astype(BF16)


def _da_tables(slopes):
    jj = np.arange(DA_TK)[:, None]
    ii = np.concatenate([np.arange(DA_TQ), np.arange(DA_TQ)])[None, :]
    dist = jnp.asarray(np.abs(ii - jj).astype(np.float32))
    dbias = -(slopes * LOG2E)[:, None, None] * dist[None]
    j = np.arange(DA_TK)
    terms = []
    for part in ((j >> 5) * 32.0, (j & 31) * 1.0):
        rem = (slopes * LOG2E)[:, None] * jnp.asarray(part.astype(np.float32))[None, :]
        for _ in range(3):
            piece = rem.astype(BF16)
            terms.append(piece)
            rem = rem - piece.astype(F32)
    pos = jnp.stack(terms, axis=-1)
    pos = jnp.concatenate([pos, jnp.zeros(pos.shape[:2] + (LANES - DA_POS_LANES,), BF16)], axis=-1)
    return dbias, pos


def _diff_attention(qkv, scal, subln_col, dbias, pos, B, S):
    T = B * S
    tq, tk = DA_TQ, DA_TK
    nq, nk = S // tq, S // tk
    kern = functools.partial(_da_kernel, nk=nk)
    return pl.pallas_call(
        kern,
        grid=(B, DA_HEADS, nq),
        in_specs=[
            pl.BlockSpec(memory_space=pltpu.SMEM),
            pl.BlockSpec((tq, LANES), lambda b, h, i: (b * nq + i, _DQ + h)),
            pl.BlockSpec((S, LANES), lambda b, h, i: (b, _DK + h)),
            pl.BlockSpec((S, LANES), lambda b, h, i: (b, _DV + h)),
            pl.BlockSpec((LANES, 1), lambda b, h, i: (0, 0)),
            pl.BlockSpec((1, tk, 2 * tq), lambda b, h, i: (h, 0, 0)),
            pl.BlockSpec((1, tk, LANES), lambda b, h, i: (h, 0, 0)),
        ],
        out_specs=pl.BlockSpec((tq, LANES), lambda b, h, i: (b * nq + i, h)),
        out_shape=jax.ShapeDtypeStruct((T, DA_WIDTH), BF16),
        scratch_shapes=[
            pltpu.VMEM((2 * LANES, 2 * tq), BF16),
            pltpu.VMEM((nk, DA_VROWS, tk), BF16),
            pltpu.VMEM((1, 2 * tq), F32),
            pltpu.VMEM((DA_VROWS, 2 * tq), F32),
            pltpu.VMEM((tk, 2 * tq), F32),
            pltpu.VMEM((tk, 2 * tq), F32),
        ],
        compiler_params=_cparams(("parallel", "parallel", "arbitrary")),
        name="diff_attn",
    )(scal, qkv, qkv, qkv, subln_col, dbias, pos)


NA_QROWS = 8
NA_KROWS = 16
NA_TQ = NA_QROWS * GRID_W
NA_TK = NA_KROWS * GRID_W


def _na_bias_tables(rpb, rows):
    cols = np.arange(GRID_W)
    cstart = np.clip(cols - NA_WIN_C // 2, 0, GRID_W - NA_WIN_C)
    kc = cols[None, :]
    col_ok = (kc >= cstart[:, None]) & (kc < cstart[:, None] + NA_WIN_C)
    dc = np.clip(kc - cols[:, None] + (NA_WIN_C - 1), 0, 2 * NA_WIN_C - 2)
    t = jnp.where(jnp.asarray(col_ok)[None, None], rpb[:, :, dc], NEG)
    t = jnp.concatenate([t, jnp.full((NA_HEADS, 1, GRID_W, GRID_W), NEG, F32)], axis=1)
    tables = []
    for r0, w0 in ((0, 0), (rows // 2 // NA_QROWS * NA_QROWS, None), (rows - NA_QROWS, rows - NA_KROWS)):
        if w0 is None:
            r0 = max(NA_QROWS, min(r0, rows - 2 * NA_QROWS))
            w0 = r0 - NA_WIN_R // 2
        r = r0 + np.arange(NA_QROWS)[:, None]
        kr = w0 + np.arange(NA_KROWS)[None, :]
        rs = np.clip(r - NA_WIN_R // 2, 0, rows - NA_WIN_R)
        ok = (kr >= rs) & (kr < rs + NA_WIN_R)
        dr = np.where(ok, kr - r + (NA_WIN_R - 1), 2 * NA_WIN_R - 1)
        tb = t[:, dr]
        tables.append(jnp.transpose(tb, (0, 1, 3, 2, 4)).reshape(NA_HEADS, NA_TQ, NA_TK))
    return jnp.stack(tables)


def _na_kernel(q_ref, k_ref, v_ref, b_ref, o_ref, *, rows):
    j = pl.program_id(2)
    w0 = jnp.clip(j * NA_QROWS - NA_WIN_R // 2, 0, rows - NA_KROWS)
    start = pl.multiple_of(w0 * GRID_W, GRID_W)
    kw = k_ref[pl.ds(start, NA_TK), :]
    vw = v_ref[pl.ds(start, NA_TK), :]
    half = NA_TQ // 2
    lane = lax.broadcasted_iota(jnp.int32, (half, LANES), 1)
    chains = [(qh, hh) for qh in range(2) for hh in range(2)]

    def scores(qh, hh):
        q = q_ref[qh * half:(qh + 1) * half, :]
        msk = (lane < HEAD_DIM) if hh == 0 else (lane >= HEAD_DIM)
        qm = jnp.where(msk, q, jnp.zeros_like(q))
        s = lax.dot_general(qm, kw, (((1,), (1,)), ((), ())), preferred_element_type=F32)
        return s + b_ref[0, hh, qh * half:(qh + 1) * half, :]

    def attend(s):
        m = jnp.max(s, axis=-1, keepdims=True)
        p = jnp.exp(s - m)
        l = jnp.sum(p, axis=-1, keepdims=True)
        return jnp.dot(p.astype(BF16), vw, preferred_element_type=F32) / l

    outs = {}
    pending = scores(*chains[0])
    for c, chain in enumerate(chains):
        s = pending
        if c + 1 < len(chains):
            pending = scores(*chains[c + 1])
        outs[chain] = attend(s)
    for qh in range(2):
        o_ref[qh * half:(qh + 1) * half, :] = jnp.where(
            lane < HEAD_DIM, outs[(qh, 0)], outs[(qh, 1)]).astype(BF16)


def _neighborhood_attention(qkv, tables, B, S):
    T = B * S
    rows = S // GRID_W
    nblk = rows // NA_QROWS
    kern = functools.partial(_na_kernel, rows=rows)

    def btype(j):
        return jnp.where(j == 0, 0, jnp.where(j == nblk - 1, 2, 1))

    return pl.pallas_call(
        kern,
        grid=(NA_HEADS // 2, B, nblk),
        in_specs=[
            pl.BlockSpec((NA_TQ, LANES), lambda p, b, j: (b * nblk + j, _NQ + p)),
            pl.BlockSpec((S, LANES), lambda p, b, j: (b, _NK + p)),
            pl.BlockSpec((S, LANES), lambda p, b, j: (b, _NV + p)),
            pl.BlockSpec((1, 2, NA_TQ, NA_TK), lambda p, b, j: (btype(j), p, 0, 0)),
        ],
        out_specs=pl.BlockSpec((NA_TQ, LANES), lambda p, b, j: (b * nblk + j, p)),
        out_shape=jax.ShapeDtypeStruct((T, NA_WIDTH), BF16),
        compiler_params=_cparams(("parallel", "parallel", "parallel")),
        name="nbr_attn",
    )(qkv, qkv, qkv, tables)


def _mix_kernel(x_ref, oda_ref, ona_ref, wo_ref, g2_ref, wqt_ref, sk_ref, x1_ref, xn_ref, st_ref):
    x1 = (x_ref[...]
          + jnp.dot(oda_ref[...], wo_ref[0:DA_WIDTH, :], preferred_element_type=F32)
          + jnp.dot(ona_ref[...], wo_ref[DA_WIDTH:, :], preferred_element_type=F32))
    x1_ref[...] = x1
    ms = jnp.mean(x1 * x1, axis=-1, keepdims=True)
    xn = (x1 * lax.rsqrt(ms + RMS_EPS) * g2_ref[...]).astype(BF16)
    xn_ref[...] = xn
    qt = lax.dot_general(wqt_ref[...], xn, (((1,), (1,)), ((), ())), preferred_element_type=F32).astype(BF16)
    for hp in range(2 * PEER_HEADS):
        st_ref[hp * PEER_KEYS:(hp + 1) * PEER_KEYS, :] = jnp.dot(
            sk_ref[hp], qt[hp * LANES:(hp + 1) * LANES, :], preferred_element_type=F32)


def _mix(x2d, oda, ona, wo_bf, g2, wqt_bf, sk_bf, tm=512):
    T = x2d.shape[0]
    nsc = 2 * PEER_HEADS * PEER_KEYS
    return pl.pallas_call(
        _mix_kernel,
        grid=(T // tm,),
        in_specs=[
            pl.BlockSpec((tm, D_MODEL), lambda i: (i, 0)),
            pl.BlockSpec((tm, DA_WIDTH), lambda i: (i, 0)),
            pl.BlockSpec((tm, NA_WIDTH), lambda i: (i, 0)),
            pl.BlockSpec((D_MODEL, D_MODEL), lambda i: (0, 0)),
            pl.BlockSpec((1, D_MODEL), lambda i: (0, 0)),
            pl.BlockSpec((nsc, D_MODEL), lambda i: (0, 0)),
            pl.BlockSpec((2 * PEER_HEADS, PEER_KEYS, LANES), lambda i: (0, 0, 0)),
        ],
        out_specs=[
            pl.BlockSpec((tm, D_MODEL), lambda i: (i, 0)),
            pl.BlockSpec((tm, D_MODEL), lambda i: (i, 0)),
            pl.BlockSpec((nsc, tm), lambda i: (0, i)),
        ],
        out_shape=[
            jax.ShapeDtypeStruct((T, D_MODEL), F32),
            jax.ShapeDtypeStruct((T, D_MODEL), BF16),
            jax.ShapeDtypeStruct((nsc, T), F32),
        ],
        compiler_params=_cparams(("parallel",)),
        name="mix",
    )(x2d, oda, ona, wo_bf, g2, wqt_bf, sk_bf)


def _top16_rows(s, nrows):
    L = s.shape[1]
    rid = lax.broadcasted_iota(jnp.int32, (nrows, L), 0).astype(F32)
    slot = lax.broadcasted_iota(jnp.int32, (PEER_TOPK, L), 0)
    vals = jnp.zeros((PEER_TOPK, L), F32)
    idxs = jnp.zeros((PEER_TOPK, L), F32)
    for it in range(PEER_TOPK):
        m = jnp.max(s, axis=0, keepdims=True)
        ix = jnp.min(jnp.where(s == m, rid, float(nrows)), axis=0, keepdims=True)
        vals = jnp.where(slot == it, m, vals)
        idxs = jnp.where(slot == it, ix, idxs)
        s = jnp.where(rid == ix, -jnp.inf, s)
    return vals, idxs


def _route_topk(st_ref, per_head):
    L = st_ref.shape[1]
    ncand = 72
    r = lax.broadcasted_iota(jnp.int32, (ncand, L), 0)
    pos = jnp.where(r < 16, r,
                    jnp.where(r < 48, (1 + ((r - 16) >> 3)) * PEER_TOPK + ((r - 16) & 7),
                              jnp.where(r < 64, (r - 48) * PEER_TOPK, (r - 64) * PEER_TOPK + 1))).astype(F32)
    dup = ((r >= 48) & (r < 53)) | ((r >= 64) & (r < 69))
    slot = lax.broadcasted_iota(jnp.int32, (PEER_TOPK, L), 0)
    slotf = slot.astype(F32)
    his, los, gates = [], [], []
    for h in range(PEER_HEADS):
        s1, i1 = _top16_rows(st_ref[(2 * h) * PEER_KEYS:(2 * h + 1) * PEER_KEYS, :], PEER_KEYS)
        s2, i2 = _top16_rows(st_ref[(2 * h + 1) * PEER_KEYS:(2 * h + 2) * PEER_KEYS, :], PEER_KEYS)
        cand = [s1[0:1, :] + s2]
        for a in range(1, 5):
            cand.append(s1[a:a + 1, :] + s2[0:8, :])
        cand.append(s1 + s2[0:1, :])
        cand.append(s1[0:8, :] + s2[1:2, :])
        cand = jnp.where(dup, -jnp.inf, jnp.concatenate(cand, axis=0))
        top = jnp.zeros((PEER_TOPK, L), F32)
        thi = jnp.zeros((PEER_TOPK, L), F32)
        tlo = jnp.zeros((PEER_TOPK, L), F32)
        for it in range(PEER_TOPK):
            m = jnp.max(cand, axis=0, keepdims=True)
            p = jnp.min(jnp.where(cand == m, pos, 1e9), axis=0, keepdims=True)
            a = jnp.floor(p * (1.0 / PEER_TOPK))
            b = p - a * PEER_TOPK
            ehi = jnp.max(jnp.where(slotf == a, i1, -1.0), axis=0, keepdims=True)
            elo = jnp.max(jnp.where(slotf == b, i2, -1.0), axis=0, keepdims=True)
            top = jnp.where(slot == it, m, top)
            thi = jnp.where(slot == it, ehi, thi)
            tlo = jnp.where(slot == it, elo, tlo)
            cand = jnp.where(pos == p, -jnp.inf, cand)
        e = jnp.exp(top - top[0:1, :])
        his.append(thi)
        los.append(tlo)
        gates.append(e / jnp.sum(e, axis=0, keepdims=True))
        per_head(h)
    return (jnp.concatenate(his, axis=0).T, jnp.concatenate(los, axis=0).T,
            jnp.concatenate(gates, axis=0).T)


GATE_ROWS = 8
ROUTE_L = 128


def _route_gates(hi_ref, lo_ref, g_ref, o_ref, t_lo, t_hi):
    nsel = hi_ref.shape[1]
    rid_b = lax.broadcasted_iota(jnp.int32, (PEER_KEYS, nsel), 0).astype(F32).astype(BF16)
    one_b = jnp.ones((PEER_KEYS, nsel), BF16)
    zero_b = jnp.zeros((PEER_KEYS, nsel), BF16)
    for t in range(t_lo, t_hi):
        hi_b = jnp.broadcast_to(hi_ref[t:t + 1, :].astype(BF16), (PEER_KEYS, nsel))
        lo_b = jnp.broadcast_to(lo_ref[t:t + 1, :].astype(BF16), (PEER_KEYS, nsel))
        g_b = jnp.broadcast_to(g_ref[t:t + 1, :].astype(BF16), (PEER_KEYS, nsel))
        qt = jnp.where(rid_b == hi_b, one_b, zero_b)
        pt = jnp.where(rid_b == lo_b, g_b, zero_b)
        gt = lax.dot_general(qt, pt, (((1,), (1,)), ((), ())), preferred_element_type=F32)
        for jb in range(PEER_EXPERTS // PEER_EBLK):
            o_ref[jb, t] = gt[jb * GATE_ROWS:(jb + 1) * GATE_ROWS, :]


def _route_kernel(st_ref, o_ref, hi_s, lo_s, g_s):
    @pl.when(pl.program_id(0) == 0)
    def _no_previous_block():
        hi_s[...] = jnp.zeros_like(hi_s)
        lo_s[...] = jnp.zeros_like(lo_s)
        g_s[...] = jnp.zeros_like(g_s)

    per = hi_s.shape[0] // PEER_HEADS

    def gates_slice(h):
        _route_gates(hi_s, lo_s, g_s, o_ref, h * per, (h + 1) * per)

    hi, lo, g = _route_topk(st_ref, gates_slice)
    hi_s[...] = hi
    lo_s[...] = lo
    g_s[...] = g


def _route(st):
    nsc, T = st.shape
    L = ROUTE_L
    nblk = T // L
    nsel = PEER_HEADS * PEER_TOPK
    njb = PEER_EXPERTS // PEER_EBLK
    return pl.pallas_call(
        _route_kernel,
        grid=(nblk + 1,),
        in_specs=[pl.BlockSpec((nsc, L), lambda i: (0, jnp.minimum(i, nblk - 1)))],
        out_specs=pl.BlockSpec((njb, L, GATE_ROWS, PEER_KEYS), lambda i: (0, jnp.maximum(i - 1, 0), 0, 0)),
        out_shape=jax.ShapeDtypeStruct((njb, T, GATE_ROWS, PEER_KEYS), F32),
        scratch_shapes=[pltpu.VMEM((L, nsel), F32)] * 3,
        compiler_params=_cparams(("arbitrary",)),
        name="peer_route",
    )(st)


PEER_TM = 1024
PEER_EBLK = 1024
assert GATE_ROWS * PEER_KEYS == PEER_EBLK


def _gelu_tanh(x):
    return 0.5 * x * (1.0 + jnp.tanh(math.sqrt(2.0 / math.pi) * (x + 0.044715 * (x * x * x))))


def _peer_kernel(x1_ref, xn_ref, ut_ref, v_ref, g3_ref, y_ref):
    j = pl.program_id(1)

    @pl.when(j == 0)
    def _residual():
        y_ref[...] = x1_ref[...]

    hmat = jnp.dot(xn_ref[...], ut_ref[...], preferred_element_type=F32)
    tm = xn_ref.shape[0]
    gates = jnp.concatenate([g3_ref[pl.ds(c, tm, stride=GATE_ROWS), :] for c in range(GATE_ROWS)], axis=1)
    w = (gates * _gelu_tanh(hmat)).astype(BF16)
    y_ref[...] += jnp.dot(w, v_ref[...], preferred_element_type=F32)


def _peer(x1, xn, ut_bf, v_bf, g3):
    T = x1.shape[0]
    tm = PEER_TM
    return pl.pallas_call(
        _peer_kernel,
        grid=(T // tm, PEER_EXPERTS // PEER_EBLK),
        in_specs=[
            pl.BlockSpec((tm, D_MODEL), lambda i, j: (i, 0)),
            pl.BlockSpec((tm, D_MODEL), lambda i, j: (i, 0)),
            pl.BlockSpec((D_MODEL, PEER_EBLK), lambda i, j: (0, j)),
            pl.BlockSpec((PEER_EBLK, D_MODEL), lambda i, j: (j, 0)),
            pl.BlockSpec((tm * GATE_ROWS, PEER_KEYS), lambda i, j: (j * (T // tm) + i, 0)),
        ],
        out_specs=pl.BlockSpec((tm, D_MODEL), lambda i, j: (i, 0)),
        out_shape=jax.ShapeDtypeStruct((T, D_MODEL), F32),
        compiler_params=_cparams(("parallel", "arbitrary")),
        name="peer_mlp",
    )(x1, xn, ut_bf, v_bf, g3.reshape(-1, PEER_KEYS))


def _prepare_params(layer_idx, ln1_g, w_in, da_q_norm_g, da_k_norm_g, lam_q1, lam_k1, lam_q2, lam_k2,
                    da_subln_g, na_q_norm_g, na_k_norm_g, na_rpb, w_out, ln2_g,
                    peer_w_query, peer_sub_keys, peer_u, peer_v):
    scale = HEAD_DIM ** -0.5
    gn = jnp.stack([jnp.tile(da_q_norm_g, 8) * (scale * LOG2E), jnp.tile(da_k_norm_g, 8),
                    jnp.tile(na_q_norm_g, 8) * scale, jnp.tile(na_k_norm_g, 8)]).astype(F32)
    grp = np.arange(512) // HEAD_DIM
    gm = jnp.asarray((grp[:, None] == grp[None, :]).astype(np.float32) / HEAD_DIM, BF16)
    lambda_init = 0.8 - 0.6 * math.exp(-0.3 * layer_idx)
    lam = (jnp.exp(jnp.sum(lam_q1.astype(F32) * lam_k1.astype(F32)))
           - jnp.exp(jnp.sum(lam_q2.astype(F32) * lam_k2.astype(F32))) + lambda_init)
    slopes = jnp.asarray([2.0 ** (-8.0 * (h + 1) / DA_HEADS) for h in range(DA_HEADS)], F32)
    scal = jnp.concatenate([slopes, lam.reshape(1), jnp.full((1,), 1.0 - lambda_init, F32)])
    dbias, pos = _da_tables(slopes)
    return dict(
        g1=ln1_g.reshape(1, D_MODEL), w_in=w_in.astype(BF16), gn=gn, gm=gm, scal=scal,
        subln=da_subln_g.reshape(LANES, 1), dbias=dbias, pos=pos, rpb=na_rpb,
        w_out=w_out.astype(BF16), g2=ln2_g.reshape(1, D_MODEL), wqt=peer_w_query.T.astype(BF16),
        sk=peer_sub_keys.reshape(2 * PEER_HEADS, PEER_KEYS, LANES).astype(BF16),
        ut=peer_u.T.astype(BF16), v=peer_v.astype(BF16))


def _encoder_layer(x, p, tables):
    B, S, _ = x.shape
    x2d = x.reshape(B * S, D_MODEL)
    qkv = _inproj(x2d, p["g1"], p["w_in"], p["gn"], p["gm"])
    oda = _diff_attention(qkv, p["scal"], p["subln"], p["dbias"], p["pos"], B, S)
    ona = _neighborhood_attention(qkv, tables, B, S)
    x1, xn, st = _mix(x2d, oda, ona, p["w_out"], p["g2"], p["wqt"], p["sk"])
    g3 = _route(st)
    y = _peer(x1, xn, p["ut"], p["v"], g3)
    return y.reshape(B, S, D_MODEL)


def kernel(x_prompt, x_sample, ln1_g, w_in, da_q_norm_g, da_k_norm_g, da_lambda_q1, da_lambda_k1,
           da_lambda_q2, da_lambda_k2, da_subln_g, na_q_norm_g, na_k_norm_g, na_rpb, w_out, ln2_g,
           peer_w_query, peer_sub_keys, peer_u, peer_v):
    hp, hs = x_prompt, x_sample
    for l in range(ln1_g.shape[0]):
        p = _prepare_params(l, ln1_g[l], w_in[l], da_q_norm_g[l], da_k_norm_g[l], da_lambda_q1[l],
                            da_lambda_k1[l], da_lambda_q2[l], da_lambda_k2[l], da_subln_g[l],
                            na_q_norm_g[l], na_k_norm_g[l], na_rpb[l], w_out[l], ln2_g[l],
                            peer_w_query[l], peer_sub_keys[l], peer_u[l], peer_v[l])
        outs = []
        for x in (hp, hs):
            tables = _na_bias_tables(p["rpb"], x.shape[1] // GRID_W)
            outs.append(_encoder_layer(x, p, tables))
        hp, hs = outs
    return (hp, hs)
```

```python
import functools
import math

import numpy as np
import jax
import jax.numpy as jnp
from jax import lax
from jax.experimental import pallas as pl
from jax.experimental.pallas import tpu as pltpu

F32 = jnp.float32
BF16 = jnp.bfloat16

D_MODEL = 1024
HEAD_DIM = 64
LANES = 128
MXU_N = 256
DA_HEADS = 4
DA_WIDTH = 512
NA_HEADS = 8
NA_WIDTH = 512
GRID_W = 64
NA_WIN_R = 8
NA_WIN_C = 16
IN_COLS = 3 * DA_WIDTH + 3 * NA_WIDTH
PEER_HEADS = 8
PEER_KEYS = 128
PEER_QDIM = 256
PEER_TOPK = 16
PEER_EXPERTS = PEER_KEYS * PEER_KEYS
RMS_EPS = 1e-6
NEG = -1e30
VMEM_LIMIT = 56 * 1024 * 1024

_DQ, _DK, _DV = 0, 4, 8
_NQ, _NK, _NV = 12, 16, 20


def _cparams(sem):
    return pltpu.CompilerParams(dimension_semantics=sem, vmem_limit_bytes=VMEM_LIMIT)


def _group_ms(x, gm):
    sq = x * x
    hi = sq.astype(BF16)
    lo = (sq - hi.astype(F32)).astype(BF16)
    return (jnp.dot(hi, gm, preferred_element_type=F32)
            + jnp.dot(lo, gm, preferred_element_type=F32))


def _inproj_kernel(x_ref, g1_ref, w_ref, gn_ref, gm_ref, o_ref):
    x = x_ref[...]
    ms = jnp.mean(x * x, axis=-1, keepdims=True)
    xn = (x * lax.rsqrt(ms + RMS_EPS) * g1_ref[...]).astype(BF16)
    proj = jnp.dot(xn, w_ref[...], preferred_element_type=F32)
    gm = gm_ref[...]
    for sec, row in ((0, 0), (1, 1), (3, 2), (4, 3)):
        xs = proj[:, sec * 512:(sec + 1) * 512]
        y = xs * lax.rsqrt(_group_ms(xs, gm) + RMS_EPS) * gn_ref[row:row + 1, :]
        o_ref[:, sec * 512:(sec + 1) * 512] = y.astype(BF16)
    for sec in (2, 5):
        o_ref[:, sec * 512:(sec + 1) * 512] = proj[:, sec * 512:(sec + 1) * 512].astype(BF16)


def _inproj(x2d, g1, w_in_bf, gn, gm, tm=512):
    T = x2d.shape[0]
    return pl.pallas_call(
        _inproj_kernel,
        grid=(T // tm,),
        in_specs=[
            pl.BlockSpec((tm, D_MODEL), lambda i: (i, 0)),
            pl.BlockSpec((1, D_MODEL), lambda i: (0, 0)),
            pl.BlockSpec((D_MODEL, IN_COLS), lambda i: (0, 0)),
            pl.BlockSpec((4, 512), lambda i: (0, 0)),
            pl.BlockSpec((512, 512), lambda i: (0, 0)),
        ],
        out_specs=pl.BlockSpec((tm, IN_COLS), lambda i: (i, 0)),
        out_shape=jax.ShapeDtypeStruct((T, IN_COLS), BF16),
        compiler_params=_cparams(("parallel",)),
        name="inproj",
    )(x2d, g1, w_in_bf, gn, gm)


DA_TQ = 512
DA_TK = 512
DA_POS_LANES = 6
DA_VROWS = 144
LOG2E = math.log2(math.e)


def _da_kernel(sc_ref, q_ref, k_ref, v_ref, g_ref, db_ref, pos_ref, o_ref,
               qst_ref, vt_ref, m_ref, acc_ref, sta_ref, stb_ref, *, nk):
    tq, tk = DA_TQ, DA_TK
    h = pl.program_id(1)
    qi = pl.program_id(2)
    slope = sc_ref[h]
    lam = sc_ref[4]
    post = sc_ref[5]

    @pl.when(qi == 0)
    def _transpose_values():
        ones_row = jnp.where(lax.broadcasted_iota(jnp.int32, (DA_VROWS - LANES, tk), 0) == 0, 1.0, 0.0)
        for c in range(nk):
            vt_ref[c, 0:LANES, :] = v_ref[c * tk:(c + 1) * tk, :].astype(F32).T.astype(BF16)
            vt_ref[c, LANES:DA_VROWS, :] = ones_row.astype(BF16)

    qT = q_ref[...].astype(F32).T
    row = lax.broadcasted_iota(jnp.int32, (LANES, tq), 0)
    qst_ref[0:LANES, 0:tq] = jnp.where(row < HEAD_DIM, qT, 0.0).astype(BF16)
    qst_ref[0:LANES, tq:2 * tq] = jnp.where(row >= HEAD_DIM, qT, 0.0).astype(BF16)
    row2 = lax.broadcasted_iota(jnp.int32, (LANES, 2 * tq), 0)
    qst_ref[LANES:2 * LANES, :] = jnp.where(row2 < DA_POS_LANES, 1.0, 0.0).astype(BF16)
    m_ref[...] = jnp.full((1, 2 * tq), NEG, F32)
    acc_ref[...] = jnp.zeros((DA_VROWS, 2 * tq), F32)

    pos = pos_ref[0].astype(F32)
    col = lax.broadcasted_iota(jnp.int32, (1, 2 * tq), 1)
    icol = jnp.where(col >= tq, col - tq, col).astype(F32)

    nch = 2 * tq // MXU_N

    def cols(n):
        return slice(n * MXU_N, (n + 1) * MXU_N)

    def side(blk):
        return jnp.where(qi > blk, 1.0, jnp.where(qi < blk, -1.0, 0.0))

    def key_operand(blk, sgn):
        k0 = pl.multiple_of(blk * tk, tk)
        return jnp.concatenate([k_ref[pl.ds(k0, tk), :], (pos * sgn).astype(BF16)], axis=1)

    def softmax_pv(n, src_ref, vtb, cq):
        sl = cols(n)
        st = src_ref[:, sl]
        c_n = cq[:, sl]
        m_old = m_ref[:, sl]
        m_new = jnp.maximum(m_old, jnp.max(st, axis=0, keepdims=True) + c_n)
        p = jnp.exp2(st - (m_new - c_n)).astype(BF16)
        alpha = jnp.exp2(m_old - m_new)
        acc_ref[:, sl] = alpha * acc_ref[:, sl] + jnp.dot(vtb, p, preferred_element_type=F32)
        m_ref[:, sl] = m_new

    def step(cur, nxt, src_ref, dst_ref):
        ka = key_operand(nxt, side(nxt))
        vtb = vt_ref[cur]
        cq = (-side(cur) * slope * LOG2E) * ((qi * tq - cur * tk).astype(F32) + icol)

        def scores(n):
            dst_ref[:, cols(n)] = jnp.dot(ka, qst_ref[:, cols(n)], preferred_element_type=F32)

        scores(0)
        scores(1)
        for n in range(nch):
            softmax_pv(n, src_ref, vtb, cq)
            if n + 2 < nch:
                scores(n + 2)

    def other(x):
        return x + (x >= qi).astype(jnp.int32)

    ka0 = key_operand(qi, 0.0)
    for n in range(nch):
        sta_ref[:, cols(n)] = (jnp.dot(ka0, qst_ref[:, cols(n)], preferred_element_type=F32)
                               + db_ref[0, :, cols(n)])

    unroll = nk if nk <= 16 else (8 if nk % 8 == 0 else 2)

    def trip(ii, carry):
        cur = jnp.where(ii == 0, qi, other(unroll * ii - 1))
        for s in range(unroll):
            nxt = other(jnp.minimum(unroll * ii + s, nk - 2))
            src_ref, dst_ref = (sta_ref, stb_ref) if s % 2 == 0 else (stb_ref, sta_ref)
            step(cur, nxt, src_ref, dst_ref)
            cur = nxt
        return carry

    lax.fori_loop(0, nk // unroll, trip, 0)

    on = acc_ref[0:LANES, :] / acc_ref[LANES:LANES + 1, :]
    oT = on[:, 0:tq] - lam * on[:, tq:2 * tq]
    ms = jnp.mean(oT * oT, axis=0, keepdims=True)
    oT = oT * lax.rsqrt(ms + RMS_EPS) * g_ref[...] * post
    o_ref[...] = oT.T.astype(BF16)


def _da_tables(slopes):
    jj = np.arange(DA_TK)[:, None]
    ii = np.concatenate([np.arange(DA_TQ), np.arange(DA_TQ)])[None, :]
    dist = jnp.asarray(np.abs(ii - jj).astype(np.float32))
    dbias = -(slopes * LOG2E)[:, None, None] * dist[None]
    j = np.arange(DA_TK)
    terms = []
    for part in ((j >> 5) * 32.0, (j & 31) * 1.0):
        rem = (slopes * LOG2E)[:, None] * jnp.asarray(part.astype(np.float32))[None, :]
        for _ in range(3):
            piece = rem.astype(BF16)
            terms.append(piece)
            rem = rem - piece.astype(F32)
    pos = jnp.stack(terms, axis=-1)
    pos = jnp.concatenate([pos, jnp.zeros(pos.shape[:2] + (LANES - DA_POS_LANES,), BF16)], axis=-1)
    return dbias, pos


def _diff_attention(qkv, scal, subln_col, dbias, pos, B, S):
    T = B * S
    tq, tk = DA_TQ, DA_TK
    nq, nk = S // tq, S // tk
    kern = functools.partial(_da_kernel, nk=nk)
    return pl.pallas_call(
        kern,
        grid=(B, DA_HEADS, nq),
        in_specs=[
            pl.BlockSpec(memory_space=pltpu.SMEM),
            pl.BlockSpec((tq, LANES), lambda b, h, i: (b * nq + i, _DQ + h)),
            pl.BlockSpec((S, LANES), lambda b, h, i: (b, _DK + h)),
            pl.BlockSpec((S, LANES), lambda b, h, i: (b, _DV + h)),
            pl.BlockSpec((LANES, 1), lambda b, h, i: (0, 0)),
            pl.BlockSpec((1, tk, 2 * tq), lambda b, h, i: (h, 0, 0)),
            pl.BlockSpec((1, tk, LANES), lambda b, h, i: (h, 0, 0)),
        ],
        out_specs=pl.BlockSpec((tq, LANES), lambda b, h, i: (b * nq + i, h)),
        out_shape=jax.ShapeDtypeStruct((T, DA_WIDTH), BF16),
        scratch_shapes=[
            pltpu.VMEM((2 * LANES, 2 * tq), BF16),
            pltpu.VMEM((nk, DA_VROWS, tk), BF16),
            pltpu.VMEM((1, 2 * tq), F32),
            pltpu.VMEM((DA_VROWS, 2 * tq), F32),
            pltpu.VMEM((tk, 2 * tq), F32),
            pltpu.VMEM((tk, 2 * tq), F32),
        ],
        compiler_params=_cparams(("parallel", "parallel", "arbitrary")),
        name="diff_attn",
    )(scal, qkv, qkv, qkv, subln_col, dbias, pos)


NA_QROWS = 8
NA_KROWS = 16
NA_TQ = NA_QROWS * GRID_W
NA_TK = NA_KROWS * GRID_W


def _na_bias_tables(rpb, rows):
    cols = np.arange(GRID_W)
    cstart = np.clip(cols - NA_WIN_C // 2, 0, GRID_W - NA_WIN_C)
    kc = cols[None, :]
    col_ok = (kc >= cstart[:, None]) & (kc < cstart[:, None] + NA_WIN_C)
    dc = np.clip(kc - cols[:, None] + (NA_WIN_C - 1), 0, 2 * NA_WIN_C - 2)
    t = jnp.where(jnp.asarray(col_ok)[None, None], rpb[:, :, dc], NEG)
    t = jnp.concatenate([t, jnp.full((NA_HEADS, 1, GRID_W, GRID_W), NEG, F32)], axis=1)
    tables = []
    for r0, w0 in ((0, 0), (rows // 2 // NA_QROWS * NA_QROWS, None), (rows - NA_QROWS, rows - NA_KROWS)):
        if w0 is None:
            r0 = max(NA_QROWS, min(r0, rows - 2 * NA_QROWS))
            w0 = r0 - NA_WIN_R // 2
        r = r0 + np.arange(NA_QROWS)[:, None]
        kr = w0 + np.arange(NA_KROWS)[None, :]
        rs = np.clip(r - NA_WIN_R // 2, 0, rows - NA_WIN_R)
        ok = (kr >= rs) & (kr < rs + NA_WIN_R)
        dr = np.where(ok, kr - r + (NA_WIN_R - 1), 2 * NA_WIN_R - 1)
        tb = t[:, dr]
        tables.append(jnp.transpose(tb, (0, 1, 3, 2, 4)).reshape(NA_HEADS, NA_TQ, NA_TK))
    return jnp.stack(tables)


def _na_kernel(q_ref, k_ref, v_ref, b_ref, o_ref, *, rows):
    j = pl.program_id(2)
    w0 = jnp.clip(j * NA_QROWS - NA_WIN_R // 2, 0, rows - NA_KROWS)
    start = pl.multiple_of(w0 * GRID_W, GRID_W)
    kw = k_ref[pl.ds(start, NA_TK), :]
    vw = v_ref[pl.ds(start, NA_TK), :]
    half = NA_TQ // 2
    lane = lax.broadcasted_iota(jnp.int32, (half, LANES), 1)
    chains = [(qh, hh) for qh in range(2) for hh in range(2)]

    def scores(qh, hh):
        q = q_ref[qh * half:(qh + 1) * half, :]
        msk = (lane < HEAD_DIM) if hh == 0 else (lane >= HEAD_DIM)
        qm = jnp.where(msk, q, jnp.zeros_like(q))
        s = lax.dot_general(qm, kw, (((1,), (1,)), ((), ())), preferred_element_type=F32)
        return s + b_ref[0, hh, qh * half:(qh + 1) * half, :]

    def attend(s):
        m = jnp.max(s, axis=-1, keepdims=True)
        p = jnp.exp(s - m)
        l = jnp.sum(p, axis=-1, keepdims=True)
        return jnp.dot(p.astype(BF16), vw, preferred_element_type=F32) / l

    outs = {}
    pending = scores(*chains[0])
    for c, chain in enumerate(chains):
        s = pending
        if c + 1 < len(chains):
            pending = scores(*chains[c + 1])
        outs[chain] = attend(s)
    for qh in range(2):
        o_ref[qh * half:(qh + 1) * half, :] = jnp.where(
            lane < HEAD_DIM, outs[(qh, 0)], outs[(qh, 1)]).astype(BF16)


def _neighborhood_attention(qkv, tables, B, S):
    T = B * S
    rows = S // GRID_W
    nblk = rows // NA_QROWS
    kern = functools.partial(_na_kernel, rows=rows)

    def btype(j):
        return jnp.where(j == 0, 0, jnp.where(j == nblk - 1, 2, 1))

    return pl.pallas_call(
        kern,
        grid=(NA_HEADS // 2, B, nblk),
        in_specs=[
            pl.BlockSpec((NA_TQ, LANES), lambda p, b, j: (b * nblk + j, _NQ + p)),
            pl.BlockSpec((S, LANES), lambda p, b, j: (b, _NK + p)),
            pl.BlockSpec((S, LANES), lambda p, b, j: (b, _NV + p)),
            pl.BlockSpec((1, 2, NA_TQ, NA_TK), lambda p, b, j: (btype(j), p, 0, 0)),
        ],
        out_specs=pl.BlockSpec((NA_TQ, LANES), lambda p, b, j: (b * nblk + j, p)),
        out_shape=jax.ShapeDtypeStruct((T, NA_WIDTH), BF16),
        compiler_params=_cparams(("parallel", "parallel", "parallel")),
        name="nbr_attn",
    )(qkv, qkv, qkv, tables)


def _mix_kernel(x_ref, oda_ref, ona_ref, wo_ref, g2_ref, wqt_ref, sk_ref, x1_ref, xn_ref, st_ref):
    x1 = (x_ref[...]
          + jnp.dot(oda_ref[...], wo_ref[0:DA_WIDTH, :], preferred_element_type=F32)
          + jnp.dot(ona_ref[...], wo_ref[DA_WIDTH:, :], preferred_element_type=F32))
    x1_ref[...] = x1
    ms = jnp.mean(x1 * x1, axis=-1, keepdims=True)
    xn = (x1 * lax.rsqrt(ms + RMS_EPS) * g2_ref[...]).astype(BF16)
    xn_ref[...] = xn
    qt = lax.dot_general(wqt_ref[...], xn, (((1,), (1,)), ((), ())), preferred_element_type=F32).astype(BF16)
    for hp in range(2 * PEER_HEADS):
        st_ref[hp * PEER_KEYS:(hp + 1) * PEER_KEYS, :] = jnp.dot(
            sk_ref[hp], qt[hp * LANES:(hp + 1) * LANES, :], preferred_element_type=F32)


def _mix(x2d, oda, ona, wo_bf, g2, wqt_bf, sk_bf, tm=512):
    T = x2d.shape[0]
    nsc = 2 * PEER_HEADS * PEER_KEYS
    return pl.pallas_call(
        _mix_kernel,
        grid=(T // tm,),
        in_specs=[
            pl.BlockSpec((tm, D_MODEL), lambda i: (i, 0)),
            pl.BlockSpec((tm, DA_WIDTH), lambda i: (i, 0)),
            pl.BlockSpec((tm, NA_WIDTH), lambda i: (i, 0)),
            pl.BlockSpec((D_MODEL, D_MODEL), lambda i: (0, 0)),
            pl.BlockSpec((1, D_MODEL), lambda i: (0, 0)),
            pl.BlockSpec((nsc, D_MODEL), lambda i: (0, 0)),
            pl.BlockSpec((2 * PEER_HEADS, PEER_KEYS, LANES), lambda i: (0, 0, 0)),
        ],
        out_specs=[
            pl.BlockSpec((tm, D_MODEL), lambda i: (i, 0)),
            pl.BlockSpec((tm, D_MODEL), lambda i: (i, 0)),
            pl.BlockSpec((nsc, tm), lambda i: (0, i)),
        ],
        out_shape=[
            jax.ShapeDtypeStruct((T, D_MODEL), F32),
            jax.ShapeDtypeStruct((T, D_MODEL), BF16),
            jax.ShapeDtypeStruct((nsc, T), F32),
        ],
        compiler_params=_cparams(("parallel",)),
        name="mix",
    )(x2d, oda, ona, wo_bf, g2, wqt_bf, sk_bf)


def _sort16_network():
    n, pairs, p = 16, [], 1
    while p < n:
        k = p
        while k >= 1:
            j = k % p
            while j <= n - 1 - k:
                for i in range(min(k - 1, n - j - k - 1) + 1):
                    if (i + j) // (2 * p) == (i + j + k) // (2 * p):
                        pairs.append((i + j, i + j + k))
                j += 2 * k
            k //= 2
        p *= 2
    return pairs


def _top16_tiles(val, idx):
    nt = len(val)
    L = val[0].shape[1]
    val, idx = list(val), list(idx)
    for a, b in [(a, b) for a, b in _sort16_network() if b < nt]:
        swap = (val[b] > val[a]) | ((val[b] == val[a]) & (idx[b] < idx[a]))
        val[a], val[b] = jnp.maximum(val[a], val[b]), jnp.minimum(val[a], val[b])
        idx[a], idx[b] = jnp.where(swap, idx[b], idx[a]), jnp.where(swap, idx[a], idx[b])
    slot = lax.broadcasted_iota(jnp.int32, (PEER_TOPK, L), 0)
    vals = jnp.zeros((PEER_TOPK, L), F32)
    idxs = jnp.zeros((PEER_TOPK, L), F32)
    for it in range(PEER_TOPK):
        m = jnp.max(val[0], axis=0, keepdims=True)
        ix = jnp.min(jnp.where(val[0] == m, idx[0], 1e9), axis=0, keepdims=True)
        vals = jnp.where(slot == it, m, vals)
        idxs = jnp.where(slot == it, ix, idxs)
        win = idx[0] == ix
        live = PEER_TOPK - 1 - it
        for k in range(min(nt - 1, live)):
            val[k] = jnp.where(win, val[k + 1], val[k])
            idx[k] = jnp.where(win, idx[k + 1], idx[k])
        if nt - 1 < live:
            val[nt - 1] = jnp.where(win, -jnp.inf, val[nt - 1])
    return vals, idxs


def _top16_rows(s):
    L = s.shape[1]
    sub = lax.broadcasted_iota(jnp.int32, (8, L), 0).astype(F32)
    ntile = s.shape[0] // 8
    return _top16_tiles([s[8 * v:8 * v + 8, :] for v in range(ntile)],
                        [sub + float(8 * v) for v in range(ntile)])


def _route_topk(st_ref, per_head):
    L = st_ref.shape[1]
    ncand = 72
    r = lax.broadcasted_iota(jnp.int32, (ncand, L), 0)
    dup = ((r >= 48) & (r < 53)) | ((r >= 64) & (r < 69))
    pos = jnp.where(r < 16, r,
                    jnp.where(r < 48, (1 + ((r - 16) >> 3)) * PEER_TOPK + ((r - 16) & 7),
                              jnp.where(r < 64, (r - 48) * PEER_TOPK, (r - 64) * PEER_TOPK + 1)))
    pos = jnp.where(dup, 1000 + r, pos).astype(F32)
    pos_tiles = [pos[8 * v:8 * v + 8, :] for v in range(ncand // 8)]
    slot = lax.broadcasted_iota(jnp.int32, (PEER_TOPK, L), 0)
    slotf = slot.astype(F32)
    his, los, gates = [], [], []
    for h in range(PEER_HEADS):
        s1, i1 = _top16_rows(st_ref[(2 * h) * PEER_KEYS:(2 * h + 1) * PEER_KEYS, :])
        s2, i2 = _top16_rows(st_ref[(2 * h + 1) * PEER_KEYS:(2 * h + 2) * PEER_KEYS, :])
        cand = [s1[0:1, :] + s2[0:8, :], s1[0:1, :] + s2[8:16, :]]
        for a in range(1, 5):
            cand.append(s1[a:a + 1, :] + s2[0:8, :])
        cand += [s1[0:8, :] + s2[0:1, :], s1[8:16, :] + s2[0:1, :], s1[0:8, :] + s2[1:2, :]]
        cand = [jnp.where(dup[8 * v:8 * v + 8, :], -jnp.inf, c) if v in (6, 8) else c
                for v, c in enumerate(cand)]
        top, tpos = _top16_tiles(cand, pos_tiles)
        ta = jnp.floor(tpos * (1.0 / PEER_TOPK))
        tb = tpos - ta * PEER_TOPK
        thi = jnp.zeros((PEER_TOPK, L), F32)
        tlo = jnp.zeros((PEER_TOPK, L), F32)
        for a in range(PEER_TOPK):
            thi = jnp.where(ta == float(a), i1[a:a + 1, :], thi)
            tlo = jnp.where(tb == float(a), i2[a:a + 1, :], tlo)
        e = jnp.exp(top - top[0:1, :])
        his.append(thi)
        los.append(tlo)
        gates.append(e / jnp.sum(e, axis=0, keepdims=True))
        per_head(h)
    return (jnp.concatenate(his, axis=0).T, jnp.concatenate(los, axis=0).T,
            jnp.concatenate(gates, axis=0).T)


GATE_ROWS = 8
ROUTE_L = 128


def _route_gates(hi_ref, lo_ref, g_ref, o_ref, t_lo, t_hi):
    nsel = hi_ref.shape[1]
    rid_b = lax.broadcasted_iota(jnp.int32, (PEER_KEYS, nsel), 0).astype(F32).astype(BF16)
    one_b = jnp.ones((PEER_KEYS, nsel), BF16)
    zero_b = jnp.zeros((PEER_KEYS, nsel), BF16)
    for t in range(t_lo, t_hi):
        hi_b = jnp.broadcast_to(hi_ref[t:t + 1, :].astype(BF16), (PEER_KEYS, nsel))
        lo_b = jnp.broadcast_to(lo_ref[t:t + 1, :].astype(BF16), (PEER_KEYS, nsel))
        g_b = jnp.broadcast_to(g_ref[t:t + 1, :].astype(BF16), (PEER_KEYS, nsel))
        qt = jnp.where(rid_b == hi_b, one_b, zero_b)
        pt = jnp.where(rid_b == lo_b, g_b, zero_b)
        gt = lax.dot_general(qt, pt, (((1,), (1,)), ((), ())), preferred_element_type=F32)
        for jb in range(PEER_EXPERTS // PEER_EBLK):
            o_ref[jb, t] = gt[jb * GATE_ROWS:(jb + 1) * GATE_ROWS, :]


def _route_kernel(st_ref, o_ref, hi_s, lo_s, g_s):
    @pl.when(pl.program_id(0) == 0)
    def _no_previous_block():
        hi_s[...] = jnp.zeros_like(hi_s)
        lo_s[...] = jnp.zeros_like(lo_s)
        g_s[...] = jnp.zeros_like(g_s)

    per = hi_s.shape[0] // PEER_HEADS

    def gates_slice(h):
        _route_gates(hi_s, lo_s, g_s, o_ref, h * per, (h + 1) * per)

    hi, lo, g = _route_topk(st_ref, gates_slice)
    hi_s[...] = hi
    lo_s[...] = lo
    g_s[...] = g


def _route(st):
    nsc, T = st.shape
    L = ROUTE_L
    nblk = T // L
    nsel = PEER_HEADS * PEER_TOPK
    njb = PEER_EXPERTS // PEER_EBLK
    return pl.pallas_call(
        _route_kernel,
        grid=(nblk + 1,),
        in_specs=[pl.BlockSpec((nsc, L), lambda i: (0, jnp.minimum(i, nblk - 1)))],
        out_specs=pl.BlockSpec((njb, L, GATE_ROWS, PEER_KEYS), lambda i: (0, jnp.maximum(i - 1, 0), 0, 0)),
        out_shape=jax.ShapeDtypeStruct((njb, T, GATE_ROWS, PEER_KEYS), F32),
        scratch_shapes=[pltpu.VMEM((L, nsel), F32)] * 3,
        compiler_params=_cparams(("arbitrary",)),
        name="peer_route",
    )(st)


PEER_TM = 1024
PEER_EBLK = 1024
assert GATE_ROWS * PEER_KEYS == PEER_EBLK


def _gelu_tanh(x):
    return 0.5 * x * (1.0 + jnp.tanh(math.sqrt(2.0 / math.pi) * (x + 0.044715 * (x * x * x))))


def _peer_kernel(x1_ref, xn_ref, ut_ref, v_ref, g3_ref, y_ref):
    j = pl.program_id(1)

    @pl.when(j == 0)
    def _residual():
        y_ref[...] = x1_ref[...]

    hmat = jnp.dot(xn_ref[...], ut_ref[...], preferred_element_type=F32)
    tm = xn_ref.shape[0]
    gates = jnp.concatenate([g3_ref[pl.ds(c, tm, stride=GATE_ROWS), :] for c in range(GATE_ROWS)], axis=1)
    w = (gates * _gelu_tanh(hmat)).astype(BF16)
    y_ref[...] += jnp.dot(w, v_ref[...], preferred_element_type=F32)


def _peer(x1, xn, ut_bf, v_bf, g3):
    T = x1.shape[0]
    tm = PEER_TM
    return pl.pallas_call(
        _peer_kernel,
        grid=(T // tm, PEER_EXPERTS // PEER_EBLK),
        in_specs=[
            pl.BlockSpec((tm, D_MODEL), lambda i, j: (i, 0)),
            pl.BlockSpec((tm, D_MODEL), lambda i, j: (i, 0)),
            pl.BlockSpec((D_MODEL, PEER_EBLK), lambda i, j: (0, j)),
            pl.BlockSpec((PEER_EBLK, D_MODEL), lambda i, j: (j, 0)),
            pl.BlockSpec((tm * GATE_ROWS, PEER_KEYS), lambda i, j: (j * (T // tm) + i, 0)),
        ],
        out_specs=pl.BlockSpec((tm, D_MODEL), lambda i, j: (i, 0)),
        out_shape=jax.ShapeDtypeStruct((T, D_MODEL), F32),
        compiler_params=_cparams(("parallel", "arbitrary")),
        name="peer_mlp",
    )(x1, xn, ut_bf, v_bf, g3.reshape(-1, PEER_KEYS))


def _prepare_params(layer_idx, ln1_g, w_in, da_q_norm_g, da_k_norm_g, lam_q1, lam_k1, lam_q2, lam_k2,
                    da_subln_g, na_q_norm_g, na_k_norm_g, na_rpb, w_out, ln2_g,
                    peer_w_query, peer_sub_keys, peer_u, peer_v):
    scale = HEAD_DIM ** -0.5
    gn = jnp.stack([jnp.tile(da_q_norm_g, 8) * (scale * LOG2E), jnp.tile(da_k_norm_g, 8),
                    jnp.tile(na_q_norm_g, 8) * scale, jnp.tile(na_k_norm_g, 8)]).astype(F32)
    grp = np.arange(512) // HEAD_DIM
    gm = jnp.asarray((grp[:, None] == grp[None, :]).astype(np.float32) / HEAD_DIM, BF16)
    lambda_init = 0.8 - 0.6 * math.exp(-0.3 * layer_idx)
    lam = (jnp.exp(jnp.sum(lam_q1.astype(F32) * lam_k1.astype(F32)))
           - jnp.exp(jnp.sum(lam_q2.astype(F32) * lam_k2.astype(F32))) + lambda_init)
    slopes = jnp.asarray([2.0 ** (-8.0 * (h + 1) / DA_HEADS) for h in range(DA_HEADS)], F32)
    scal = jnp.concatenate([slopes, lam.reshape(1), jnp.full((1,), 1.0 - lambda_init, F32)])
    dbias, pos = _da_tables(slopes)
    return dict(
        g1=ln1_g.reshape(1, D_MODEL), w_in=w_in.astype(BF16), gn=gn, gm=gm, scal=scal,
        subln=da_subln_g.reshape(LANES, 1), dbias=dbias, pos=pos, rpb=na_rpb,
        w_out=w_out.astype(BF16), g2=ln2_g.reshape(1, D_MODEL), wqt=peer_w_query.T.astype(BF16),
        sk=peer_sub_keys.reshape(2 * PEER_HEADS, PEER_KEYS, LANES).astype(BF16),
        ut=peer_u.T.astype(BF16), v=peer_v.astype(BF16))


def _encoder_layer(x, p, tables):
    B, S, _ = x.shape
    x2d = x.reshape(B * S, D_MODEL)
    qkv = _inproj(x2d, p["g1"], p["w_in"], p["gn"], p["gm"])
    oda = _diff_attention(qkv, p["scal"], p["subln"], p["dbias"], p["pos"], B, S)
    ona = _neighborhood_attention(qkv, tables, B, S)
    x1, xn, st = _mix(x2d, oda, ona, p["w_out"], p["g2"], p["wqt"], p["sk"])
    g3 = _route(st)
    y = _peer(x1, xn, p["ut"], p["v"], g3)
    return y.reshape(B, S, D_MODEL)


def kernel(x_prompt, x_sample, ln1_g, w_in, da_q_norm_g, da_k_norm_g, da_lambda_q1, da_lambda_k1,
           da_lambda_q2, da_lambda_k2, da_subln_g, na_q_norm_g, na_k_norm_g, na_rpb, w_out, ln2_g,
           peer_w_query, peer_sub_keys, peer_u, peer_v):
    hp, hs = x_prompt, x_sample
    for l in range(ln1_g.shape[0]):
        p = _prepare_params(l, ln1_g[l], w_in[l], da_q_norm_g[l], da_k_norm_g[l], da_lambda_q1[l],
                            da_lambda_k1[l], da_lambda_q2[l], da_lambda_k2[l], da_subln_g[l],
                            na_q_norm_g[l], na_k_norm_g[l], na_rpb[l], w_out[l], ln2_g[l],
                            peer_w_query[l], peer_sub_keys[l], peer_u[l], peer_v[l])
        outs = []
        for x in (hp, hs):
            tables = _na_bias_tables(p["rpb"], x.shape[1] // GRID_W)
            outs.append(_encoder_layer(x, p, tables))
        hp, hs = outs
    return (hp, hs)
```

```python
import functools
import math

import numpy as np
import jax
import jax.numpy as jnp
from jax import lax
from jax.experimental import pallas as pl
from jax.experimental.pallas import tpu as pltpu

F32 = jnp.float32
BF16 = jnp.bfloat16

D_MODEL = 1024
HEAD_DIM = 64
LANES = 128
MXU_N = 256
DA_HEADS = 4
DA_WIDTH = 512
NA_HEADS = 8
NA_WIDTH = 512
GRID_W = 64
NA_WIN_R = 8
NA_WIN_C = 16
IN_COLS = 3 * DA_WIDTH + 3 * NA_WIDTH
PEER_HEADS = 8
PEER_KEYS = 128
PEER_QDIM = 256
PEER_TOPK = 16
PEER_EXPERTS = PEER_KEYS * PEER_KEYS
RMS_EPS = 1e-6
NEG = -1e30
VMEM_LIMIT = 56 * 1024 * 1024

_DQ, _DK, _DV = 0, 4, 8
_NQ, _NK, _NV = 12, 16, 20


def _cparams(sem):
    return pltpu.CompilerParams(dimension_semantics=sem, vmem_limit_bytes=VMEM_LIMIT)


def _group_ms(x, gm):
    sq = x * x
    hi = sq.astype(BF16)
    lo = (sq - hi.astype(F32)).astype(BF16)
    return (jnp.dot(hi, gm, preferred_element_type=F32)
            + jnp.dot(lo, gm, preferred_element_type=F32))


def _inproj_kernel(x_ref, g1_ref, w_ref, gn_ref, gm_ref, o_ref):
    x = x_ref[...]
    ms = jnp.mean(x * x, axis=-1, keepdims=True)
    xn = (x * lax.rsqrt(ms + RMS_EPS) * g1_ref[...]).astype(BF16)
    proj = jnp.dot(xn, w_ref[...], preferred_element_type=F32)
    gm = gm_ref[...]
    for sec, row in ((0, 0), (1, 1), (3, 2), (4, 3)):
        xs = proj[:, sec * 512:(sec + 1) * 512]
        y = xs * lax.rsqrt(_group_ms(xs, gm) + RMS_EPS) * gn_ref[row:row + 1, :]
        o_ref[:, sec * 512:(sec + 1) * 512] = y.astype(BF16)
    for sec in (2, 5):
        o_ref[:, sec * 512:(sec + 1) * 512] = proj[:, sec * 512:(sec + 1) * 512].astype(BF16)


def _inproj(x2d, g1, w_in_bf, gn, gm, tm=512):
    T = x2d.shape[0]
    return pl.pallas_call(
        _inproj_kernel,
        grid=(T // tm,),
        in_specs=[
            pl.BlockSpec((tm, D_MODEL), lambda i: (i, 0)),
            pl.BlockSpec((1, D_MODEL), lambda i: (0, 0)),
            pl.BlockSpec((D_MODEL, IN_COLS), lambda i: (0, 0)),
            pl.BlockSpec((4, 512), lambda i: (0, 0)),
            pl.BlockSpec((512, 512), lambda i: (0, 0)),
        ],
        out_specs=pl.BlockSpec((tm, IN_COLS), lambda i: (i, 0)),
        out_shape=jax.ShapeDtypeStruct((T, IN_COLS), BF16),
        compiler_params=_cparams(("parallel",)),
        name="inproj",
    )(x2d, g1, w_in_bf, gn, gm)


DA_TQ = 512
DA_TK = 512
DA_CHUNK = MXU_N
DA_POS_LANES = 6
DA_VROWS = 144
LOG2E = math.log2(math.e)


def _da_kernel(sc_ref, q_ref, qn_ref, k_ref, v_ref, g_ref, db_ref, pos_ref, o_ref,
               qst_ref, qstn_ref, vt_ref, m_ref, acc_ref, sta_ref, stb_ref, *, nk):
    tq, tk = DA_TQ, DA_TK
    h = pl.program_id(1)
    qi = pl.program_id(2)
    nq = pl.num_programs(2)
    slope = sc_ref[h]
    lam = sc_ref[4]
    post = sc_ref[5]
    unroll = nk if nk <= 16 else (8 if nk % 8 == 0 else 2)
    handoff = unroll == nk

    @pl.when(qi == 0)
    def _transpose_values():
        ones_row = jnp.where(lax.broadcasted_iota(jnp.int32, (DA_VROWS - LANES, tk), 0) == 0, 1.0, 0.0)
        for c in range(nk):
            vt_ref[c, 0:LANES, :] = v_ref[c * tk:(c + 1) * tk, :].astype(F32).T.astype(BF16)
            vt_ref[c, LANES:DA_VROWS, :] = ones_row.astype(BF16)

    def query_operand(dst_ref, src_ref):
        qT = src_ref[...].astype(F32).T
        row = lax.broadcasted_iota(jnp.int32, (LANES, tq), 0)
        dst_ref[0:LANES, 0:tq] = jnp.where(row < HEAD_DIM, qT, 0.0).astype(BF16)
        dst_ref[0:LANES, tq:2 * tq] = jnp.where(row >= HEAD_DIM, qT, 0.0).astype(BF16)
        row2 = lax.broadcasted_iota(jnp.int32, (LANES, 2 * tq), 0)
        dst_ref[LANES:2 * LANES, :] = jnp.where(row2 < DA_POS_LANES, 1.0, 0.0).astype(BF16)

    query_operand(qst_ref, q_ref)
    if handoff:
        query_operand(qstn_ref, qn_ref)
    m_ref[...] = jnp.full((1, 2 * tq), NEG, F32)
    acc_ref[...] = jnp.zeros((DA_VROWS, 2 * tq), F32)

    pos = pos_ref[0].astype(F32)
    col = lax.broadcasted_iota(jnp.int32, (1, 2 * tq), 1)
    icol = jnp.where(col >= tq, col - tq, col).astype(F32)

    nch = 2 * tq // DA_CHUNK

    def cols(n):
        return slice(n * DA_CHUNK, (n + 1) * DA_CHUNK)

    def side(blk):
        return jnp.where(qi > blk, 1.0, jnp.where(qi < blk, -1.0, 0.0))

    def key_operand(blk, sgn):
        k0 = pl.multiple_of(blk * tk, tk)
        return jnp.concatenate([k_ref[pl.ds(k0, tk), :], (pos * sgn).astype(BF16)], axis=1)

    def softmax_pv(n, src_ref, vtb, cq):
        sl = cols(n)
        st = src_ref[:, sl]
        c_n = cq[:, sl]
        m_old = m_ref[:, sl]
        m_new = jnp.maximum(m_old, jnp.max(st, axis=0, keepdims=True) + c_n)
        p = jnp.exp2(st - (m_new - c_n)).astype(BF16)
        alpha = jnp.exp2(m_old - m_new)
        acc_ref[:, sl] = alpha * acc_ref[:, sl] + jnp.dot(vtb, p, preferred_element_type=F32)
        m_ref[:, sl] = m_new

    def diagonal_scores(n, ka, qop_ref):
        return jnp.dot(ka, qop_ref[:, cols(n)], preferred_element_type=F32) + db_ref[0, :, cols(n)]

    def step(cur, nxt, src_ref, dst_ref, next_query=False):
        ka = key_operand(nxt, 0.0 if next_query else side(nxt))
        vtb = vt_ref[cur]
        cq = (-side(cur) * slope * LOG2E) * ((qi * tq - cur * tk).astype(F32) + icol)

        def scores(n):
            if next_query:
                dst_ref[:, cols(n)] = diagonal_scores(n, ka, qstn_ref)
            else:
                dst_ref[:, cols(n)] = jnp.dot(ka, qst_ref[:, cols(n)], preferred_element_type=F32)

        scores(0)
        scores(1)
        for n in range(nch):
            softmax_pv(n, src_ref, vtb, cq)
            if n + 2 < nch:
                scores(n + 2)

    def other(x):
        return x + (x >= qi).astype(jnp.int32)

    def first_scores():
        ka0 = key_operand(qi, 0.0)
        for n in range(nch):
            sta_ref[:, cols(n)] = diagonal_scores(n, ka0, qst_ref)

    if handoff:
        pl.when(qi == 0)(first_scores)
        cur = qi
        for s in range(nk):
            src_ref, dst_ref = (sta_ref, stb_ref) if s % 2 == 0 else (stb_ref, sta_ref)
            if s < nk - 1:
                nxt = other(s)
                step(cur, nxt, src_ref, dst_ref)
                cur = nxt
            else:
                step(cur, jnp.minimum(qi + 1, nq - 1), src_ref, dst_ref, next_query=True)
    else:
        first_scores()

        def trip(ii, carry):
            cur = jnp.where(ii == 0, qi, other(unroll * ii - 1))
            for s in range(unroll):
                nxt = other(jnp.minimum(unroll * ii + s, nk - 2))
                src_ref, dst_ref = (sta_ref, stb_ref) if s % 2 == 0 else (stb_ref, sta_ref)
                step(cur, nxt, src_ref, dst_ref)
                cur = nxt
            return carry

        lax.fori_loop(0, nk // unroll, trip, 0)

    on = acc_ref[0:LANES, :] / acc_ref[LANES:LANES + 1, :]
    oT = on[:, 0:tq] - lam * on[:, tq:2 * tq]
    ms = jnp.mean(oT * oT, axis=0, keepdims=True)
    oT = oT * lax.rsqrt(ms + RMS_EPS) * g_ref[...] * post
    o_ref[...] = oT.T.astype(BF16)


def _da_tables(slopes):
    jj = np.arange(DA_TK)[:, None]
    ii = np.concatenate([np.arange(DA_TQ), np.arange(DA_TQ)])[None, :]
    dist = jnp.asarray(np.abs(ii - jj).astype(np.float32))
    dbias = -(slopes * LOG2E)[:, None, None] * dist[None]
    j = np.arange(DA_TK)
    terms = []
    for part in ((j >> 5) * 32.0, (j & 31) * 1.0):
        rem = (slopes * LOG2E)[:, None] * jnp.asarray(part.astype(np.float32))[None, :]
        for _ in range(3):
            piece = rem.astype(BF16)
            terms.append(piece)
            rem = rem - piece.astype(F32)
    pos = jnp.stack(terms, axis=-1)
    pos = jnp.concatenate([pos, jnp.zeros(pos.shape[:2] + (LANES - DA_POS_LANES,), BF16)], axis=-1)
    return dbias, pos


def _diff_attention(qkv, scal, subln_col, dbias, pos, B, S):
    T = B * S
    tq, tk = DA_TQ, DA_TK
    nq, nk = S // tq, S // tk
    kern = functools.partial(_da_kernel, nk=nk)
    return pl.pallas_call(
        kern,
        grid=(B, DA_HEADS, nq),
        in_specs=[
            pl.BlockSpec(memory_space=pltpu.SMEM),
            pl.BlockSpec((tq, LANES), lambda b, h, i: (b * nq + i, _DQ + h)),
            pl.BlockSpec((tq, LANES), lambda b, h, i: (b * nq + jnp.minimum(i + 1, nq - 1), _DQ + h)),
            pl.BlockSpec((S, LANES), lambda b, h, i: (b, _DK + h)),
            pl.BlockSpec((S, LANES), lambda b, h, i: (b, _DV + h)),
            pl.BlockSpec((LANES, 1), lambda b, h, i: (0, 0)),
            pl.BlockSpec((1, tk, 2 * tq), lambda b, h, i: (h, 0, 0)),
            pl.BlockSpec((1, tk, LANES), lambda b, h, i: (h, 0, 0)),
        ],
        out_specs=pl.BlockSpec((tq, LANES), lambda b, h, i: (b * nq + i, h)),
        out_shape=jax.ShapeDtypeStruct((T, DA_WIDTH), BF16),
        scratch_shapes=[
            pltpu.VMEM((2 * LANES, 2 * tq), BF16),
            pltpu.VMEM((2 * LANES, 2 * tq), BF16),
            pltpu.VMEM((nk, DA_VROWS, tk), BF16),
            pltpu.VMEM((1, 2 * tq), F32),
            pltpu.VMEM((DA_VROWS, 2 * tq), F32),
            pltpu.VMEM((tk, 2 * tq), F32),
            pltpu.VMEM((tk, 2 * tq), F32),
        ],
        compiler_params=_cparams(("parallel", "parallel", "arbitrary")),
        name="diff_attn",
    )(scal, qkv, qkv, qkv, qkv, subln_col, dbias, pos)


NA_QROWS = 8
NA_KROWS = 16
NA_TQ = NA_QROWS * GRID_W
NA_TK = NA_KROWS * GRID_W


def _na_bias_tables(rpb, rows):
    cols = np.arange(GRID_W)
    cstart = np.clip(cols - NA_WIN_C // 2, 0, GRID_W - NA_WIN_C)
    kc = cols[None, :]
    col_ok = (kc >= cstart[:, None]) & (kc < cstart[:, None] + NA_WIN_C)
    dc = np.clip(kc - cols[:, None] + (NA_WIN_C - 1), 0, 2 * NA_WIN_C - 2)
    t = jnp.where(jnp.asarray(col_ok)[None, None], rpb[:, :, dc], NEG)
    t = jnp.concatenate([t, jnp.full((NA_HEADS, 1, GRID_W, GRID_W), NEG, F32)], axis=1)
    tables = []
    for r0, w0 in ((0, 0), (rows // 2 // NA_QROWS * NA_QROWS, None), (rows - NA_QROWS, rows - NA_KROWS)):
        if w0 is None:
            r0 = max(NA_QROWS, min(r0, rows - 2 * NA_QROWS))
            w0 = r0 - NA_WIN_R // 2
        r = r0 + np.arange(NA_QROWS)[:, None]
        kr = w0 + np.arange(NA_KROWS)[None, :]
        rs = np.clip(r - NA_WIN_R // 2, 0, rows - NA_WIN_R)
        ok = (kr >= rs) & (kr < rs + NA_WIN_R)
        dr = np.where(ok, kr - r + (NA_WIN_R - 1), 2 * NA_WIN_R - 1)
        tb = t[:, dr]
        tables.append(jnp.transpose(tb, (0, 1, 3, 2, 4)).reshape(NA_HEADS, NA_TQ, NA_TK))
    return jnp.stack(tables)


def _na_kernel(q_ref, k_ref, v_ref, b_ref, o_ref, *, rows):
    j = pl.program_id(2)
    w0 = jnp.clip(j * NA_QROWS - NA_WIN_R // 2, 0, rows - NA_KROWS)
    start = pl.multiple_of(w0 * GRID_W, GRID_W)
    kw = k_ref[pl.ds(start, NA_TK), :]
    vw = v_ref[pl.ds(start, NA_TK), :]
    half = NA_TQ // 2
    lane = lax.broadcasted_iota(jnp.int32, (half, LANES), 1)
    chains = [(qh, hh) for qh in range(2) for hh in range(2)]

    def scores(qh, hh):
        q = q_ref[qh * half:(qh + 1) * half, :]
        msk = (lane < HEAD_DIM) if hh == 0 else (lane >= HEAD_DIM)
        qm = jnp.where(msk, q, jnp.zeros_like(q))
        s = lax.dot_general(qm, kw, (((1,), (1,)), ((), ())), preferred_element_type=F32)
        return s + b_ref[0, hh, qh * half:(qh + 1) * half, :]

    def attend(s):
        m = jnp.max(s, axis=-1, keepdims=True)
        p = jnp.exp(s - m)
        l = jnp.sum(p, axis=-1, keepdims=True)
        return jnp.dot(p.astype(BF16), vw, preferred_element_type=F32) / l

    outs = {}
    pending = scores(*chains[0])
    for c, chain in enumerate(chains):
        s = pending
        if c + 1 < len(chains):
            pending = scores(*chains[c + 1])
        outs[chain] = attend(s)
    for qh in range(2):
        o_ref[qh * half:(qh + 1) * half, :] = jnp.where(
            lane < HEAD_DIM, outs[(qh, 0)], outs[(qh, 1)]).astype(BF16)


def _neighborhood_attention(qkv, tables, B, S):
    T = B * S
    rows = S // GRID_W
    nblk = rows // NA_QROWS
    kern = functools.partial(_na_kernel, rows=rows)

    def btype(j):
        return jnp.where(j == 0, 0, jnp.where(j == nblk - 1, 2, 1))

    return pl.pallas_call(
        kern,
        grid=(NA_HEADS // 2, B, nblk),
        in_specs=[
            pl.BlockSpec((NA_TQ, LANES), lambda p, b, j: (b * nblk + j, _NQ + p)),
            pl.BlockSpec((S, LANES), lambda p, b, j: (b, _NK + p)),
            pl.BlockSpec((S, LANES), lambda p, b, j: (b, _NV + p)),
            pl.BlockSpec((1, 2, NA_TQ, NA_TK), lambda p, b, j: (btype(j), p, 0, 0)),
        ],
        out_specs=pl.BlockSpec((NA_TQ, LANES), lambda p, b, j: (b * nblk + j, p)),
        out_shape=jax.ShapeDtypeStruct((T, NA_WIDTH), BF16),
        compiler_params=_cparams(("parallel", "parallel", "parallel")),
        name="nbr_attn",
    )(qkv, qkv, qkv, tables)


def _mix_kernel(x_ref, oda_ref, ona_ref, wo_ref, g2_ref, wqt_ref, sk_ref, x1_ref, xn_ref, st_ref):
    x1 = (x_ref[...]
          + jnp.dot(oda_ref[...], wo_ref[0:DA_WIDTH, :], preferred_element_type=F32)
          + jnp.dot(ona_ref[...], wo_ref[DA_WIDTH:, :], preferred_element_type=F32))
    x1_ref[...] = x1
    ms = jnp.mean(x1 * x1, axis=-1, keepdims=True)
    xn = (x1 * lax.rsqrt(ms + RMS_EPS) * g2_ref[...]).astype(BF16)
    xn_ref[...] = xn
    qt = lax.dot_general(wqt_ref[...], xn, (((1,), (1,)), ((), ())), preferred_element_type=F32).astype(BF16)
    for hp in range(2 * PEER_HEADS):
        st_ref[hp * PEER_KEYS:(hp + 1) * PEER_KEYS, :] = jnp.dot(
            sk_ref[hp], qt[hp * LANES:(hp + 1) * LANES, :], preferred_element_type=F32)


def _mix(x2d, oda, ona, wo_bf, g2, wqt_bf, sk_bf, tm=512):
    T = x2d.shape[0]
    nsc = 2 * PEER_HEADS * PEER_KEYS
    return pl.pallas_call(
        _mix_kernel,
        grid=(T // tm,),
        in_specs=[
            pl.BlockSpec((tm, D_MODEL), lambda i: (i, 0)),
            pl.BlockSpec((tm, DA_WIDTH), lambda i: (i, 0)),
            pl.BlockSpec((tm, NA_WIDTH), lambda i: (i, 0)),
            pl.BlockSpec((D_MODEL, D_MODEL), lambda i: (0, 0)),
            pl.BlockSpec((1, D_MODEL), lambda i: (0, 0)),
            pl.BlockSpec((nsc, D_MODEL), lambda i: (0, 0)),
            pl.BlockSpec((2 * PEER_HEADS, PEER_KEYS, LANES), lambda i: (0, 0, 0)),
        ],
        out_specs=[
            pl.BlockSpec((tm, D_MODEL), lambda i: (i, 0)),
            pl.BlockSpec((tm, D_MODEL), lambda i: (i, 0)),
            pl.BlockSpec((nsc, tm), lambda i: (0, i)),
        ],
        out_shape=[
            jax.ShapeDtypeStruct((T, D_MODEL), F32),
            jax.ShapeDtypeStruct((T, D_MODEL), BF16),
            jax.ShapeDtypeStruct((nsc, T), F32),
        ],
        compiler_params=_cparams(("parallel",)),
        name="mix",
    )(x2d, oda, ona, wo_bf, g2, wqt_bf, sk_bf)


def _sort16_network():
    n, pairs, p = 16, [], 1
    while p < n:
        k = p
        while k >= 1:
            j = k % p
            while j <= n - 1 - k:
                for i in range(min(k - 1, n - j - k - 1) + 1):
                    if (i + j) // (2 * p) == (i + j + k) // (2 * p):
                        pairs.append((i + j, i + j + k))
                j += 2 * k
            k //= 2
        p *= 2
    return pairs


def _top16_tiles(val, idx):
    nt = len(val)
    L = val[0].shape[1]
    val, idx = list(val), list(idx)
    for a, b in [(a, b) for a, b in _sort16_network() if b < nt]:
        swap = (val[b] > val[a]) | ((val[b] == val[a]) & (idx[b] < idx[a]))
        val[a], val[b] = jnp.maximum(val[a], val[b]), jnp.minimum(val[a], val[b])
        idx[a], idx[b] = jnp.where(swap, idx[b], idx[a]), jnp.where(swap, idx[a], idx[b])
    slot = lax.broadcasted_iota(jnp.int32, (PEER_TOPK, L), 0)
    vals = jnp.zeros((PEER_TOPK, L), F32)
    idxs = jnp.zeros((PEER_TOPK, L), F32)
    for it in range(PEER_TOPK):
        m = jnp.max(val[0], axis=0, keepdims=True)
        ix = jnp.min(jnp.where(val[0] == m, idx[0], 1e9), axis=0, keepdims=True)
        vals = jnp.where(slot == it, m, vals)
        idxs = jnp.where(slot == it, ix, idxs)
        win = idx[0] == ix
        live = PEER_TOPK - 1 - it
        for k in range(min(nt - 1, live)):
            val[k] = jnp.where(win, val[k + 1], val[k])
            idx[k] = jnp.where(win, idx[k + 1], idx[k])
        if nt - 1 < live:
            val[nt - 1] = jnp.where(win, -jnp.inf, val[nt - 1])
    return vals, idxs


def _top16_rows(s):
    L = s.shape[1]
    sub = lax.broadcasted_iota(jnp.int32, (8, L), 0).astype(F32)
    ntile = s.shape[0] // 8
    return _top16_tiles([s[8 * v:8 * v + 8, :] for v in range(ntile)],
                        [sub + float(8 * v) for v in range(ntile)])


def _route_topk(st_ref, per_head):
    L = st_ref.shape[1]
    ncand = 72
    r = lax.broadcasted_iota(jnp.int32, (ncand, L), 0)
    dup = ((r >= 48) & (r < 53)) | ((r >= 64) & (r < 69))
    pos = jnp.where(r < 16, r,
                    jnp.where(r < 48, (1 + ((r - 16) >> 3)) * PEER_TOPK + ((r - 16) & 7),
                              jnp.where(r < 64, (r - 48) * PEER_TOPK, (r - 64) * PEER_TOPK + 1)))
    pos = jnp.where(dup, 1000 + r, pos).astype(F32)
    pos_tiles = [pos[8 * v:8 * v + 8, :] for v in range(ncand // 8)]
    slot = lax.broadcasted_iota(jnp.int32, (PEER_TOPK, L), 0)
    slotf = slot.astype(F32)
    his, los, gates = [], [], []
    for h in range(PEER_HEADS):
        s1, i1 = _top16_rows(st_ref[(2 * h) * PEER_KEYS:(2 * h + 1) * PEER_KEYS, :])
        s2, i2 = _top16_rows(st_ref[(2 * h + 1) * PEER_KEYS:(2 * h + 2) * PEER_KEYS, :])
        cand = [s1[0:1, :] + s2[0:8, :], s1[0:1, :] + s2[8:16, :]]
        for a in range(1, 5):
            cand.append(s1[a:a + 1, :] + s2[0:8, :])
        cand += [s1[0:8, :] + s2[0:1, :], s1[8:16, :] + s2[0:1, :], s1[0:8, :] + s2[1:2, :]]
        cand = [jnp.where(dup[8 * v:8 * v + 8, :], -jnp.inf, c) if v in (6, 8) else c
                for v, c in enumerate(cand)]
        top, tpos = _top16_tiles(cand, pos_tiles)
        ta = jnp.floor(tpos * (1.0 / PEER_TOPK))
        tb = tpos - ta * PEER_TOPK
        thi = jnp.zeros((PEER_TOPK, L), F32)
        tlo = jnp.zeros((PEER_TOPK, L), F32)
        for a in range(PEER_TOPK):
            thi = jnp.where(ta == float(a), i1[a:a + 1, :], thi)
            tlo = jnp.where(tb == float(a), i2[a:a + 1, :], tlo)
        e = jnp.exp(top - top[0:1, :])
        his.append(thi)
        los.append(tlo)
        gates.append(e / jnp.sum(e, axis=0, keepdims=True))
        per_head(h)
    return (jnp.concatenate(his, axis=0).T, jnp.concatenate(los, axis=0).T,
            jnp.concatenate(gates, axis=0).T)


GATE_ROWS = 8
ROUTE_L = 128


def _route_gates(hi_ref, lo_ref, g_ref, o_ref, t_lo, t_hi):
    nsel = hi_ref.shape[1]
    rid_b = lax.broadcasted_iota(jnp.int32, (PEER_KEYS, nsel), 0).astype(F32).astype(BF16)
    one_b = jnp.ones((PEER_KEYS, nsel), BF16)
    zero_b = jnp.zeros((PEER_KEYS, nsel), BF16)
    for t in range(t_lo, t_hi):
        hi_b = jnp.broadcast_to(hi_ref[t:t + 1, :].astype(BF16), (PEER_KEYS, nsel))
        lo_b = jnp.broadcast_to(lo_ref[t:t + 1, :].astype(BF16), (PEER_KEYS, nsel))
        g_b = jnp.broadcast_to(g_ref[t:t + 1, :].astype(BF16), (PEER_KEYS, nsel))
        qt = jnp.where(rid_b == hi_b, one_b, zero_b)
        pt = jnp.where(rid_b == lo_b, g_b, zero_b)
        gt = lax.dot_general(qt, pt, (((1,), (1,)), ((), ())), preferred_element_type=F32)
        for jb in range(PEER_EXPERTS // PEER_EBLK):
            o_ref[jb, t] = gt[jb * GATE_ROWS:(jb + 1) * GATE_ROWS, :]


def _route_kernel(st_ref, o_ref, hi_s, lo_s, g_s):
    @pl.when(pl.program_id(0) == 0)
    def _no_previous_block():
        hi_s[...] = jnp.zeros_like(hi_s)
        lo_s[...] = jnp.zeros_like(lo_s)
        g_s[...] = jnp.zeros_like(g_s)

    per = hi_s.shape[0] // PEER_HEADS

    def gates_slice(h):
        _route_gates(hi_s, lo_s, g_s, o_ref, h * per, (h + 1) * per)

    hi, lo, g = _route_topk(st_ref, gates_slice)
    hi_s[...] = hi
    lo_s[...] = lo
    g_s[...] = g


def _route(st):
    nsc, T = st.shape
    L = ROUTE_L
    nblk = T // L
    nsel = PEER_HEADS * PEER_TOPK
    njb = PEER_EXPERTS // PEER_EBLK
    return pl.pallas_call(
        _route_kernel,
        grid=(nblk + 1,),
        in_specs=[pl.BlockSpec((nsc, L), lambda i: (0, jnp.minimum(i, nblk - 1)))],
        out_specs=pl.BlockSpec((njb, L, GATE_ROWS, PEER_KEYS), lambda i: (0, jnp.maximum(i - 1, 0), 0, 0)),
        out_shape=jax.ShapeDtypeStruct((njb, T, GATE_ROWS, PEER_KEYS), F32),
        scratch_shapes=[pltpu.VMEM((L, nsel), F32)] * 3,
        compiler_params=_cparams(("arbitrary",)),
        name="peer_route",
    )(st)


PEER_TM = 1024
PEER_EBLK = 1024
assert GATE_ROWS * PEER_KEYS == PEER_EBLK


def _gelu_tanh(x):
    return 0.5 * x * (1.0 + jnp.tanh(math.sqrt(2.0 / math.pi) * (x + 0.044715 * (x * x * x))))


def _peer_kernel(x1_ref, xn_ref, ut_ref, v_ref, g3_ref, y_ref):
    j = pl.program_id(1)

    @pl.when(j == 0)
    def _residual():
        y_ref[...] = x1_ref[...]

    hmat = jnp.dot(xn_ref[...], ut_ref[...], preferred_element_type=F32)
    tm = xn_ref.shape[0]
    gates = jnp.concatenate([g3_ref[pl.ds(c, tm, stride=GATE_ROWS), :] for c in range(GATE_ROWS)], axis=1)
    w = (gates * _gelu_tanh(hmat)).astype(BF16)
    y_ref[...] += jnp.dot(w, v_ref[...], preferred_element_type=F32)


def _peer(x1, xn, ut_bf, v_bf, g3):
    T = x1.shape[0]
    tm = PEER_TM
    return pl.pallas_call(
        _peer_kernel,
        grid=(T // tm, PEER_EXPERTS // PEER_EBLK),
        in_specs=[
            pl.BlockSpec((tm, D_MODEL), lambda i, j: (i, 0)),
            pl.BlockSpec((tm, D_MODEL), lambda i, j: (i, 0)),
            pl.BlockSpec((D_MODEL, PEER_EBLK), lambda i, j: (0, j)),
            pl.BlockSpec((PEER_EBLK, D_MODEL), lambda i, j: (j, 0)),
            pl.BlockSpec((tm * GATE_ROWS, PEER_KEYS), lambda i, j: (j * (T // tm) + i, 0)),
        ],
        out_specs=pl.BlockSpec((tm, D_MODEL), lambda i, j: (i, 0)),
        out_shape=jax.ShapeDtypeStruct((T, D_MODEL), F32),
        compiler_params=_cparams(("parallel", "arbitrary")),
        name="peer_mlp",
    )(x1, xn, ut_bf, v_bf, g3.reshape(-1, PEER_KEYS))


def _prepare_params(layer_idx, ln1_g, w_in, da_q_norm_g, da_k_norm_g, lam_q1, lam_k1, lam_q2, lam_k2,
                    da_subln_g, na_q_norm_g, na_k_norm_g, na_rpb, w_out, ln2_g,
                    peer_w_query, peer_sub_keys, peer_u, peer_v):
    scale = HEAD_DIM ** -0.5
    gn = jnp.stack([jnp.tile(da_q_norm_g, 8) * (scale * LOG2E), jnp.tile(da_k_norm_g, 8),
                    jnp.tile(na_q_norm_g, 8) * scale, jnp.tile(na_k_norm_g, 8)]).astype(F32)
    grp = np.arange(512) // HEAD_DIM
    gm = jnp.asarray((grp[:, None] == grp[None, :]).astype(np.float32) / HEAD_DIM, BF16)
    lambda_init = 0.8 - 0.6 * math.exp(-0.3 * layer_idx)
    lam = (jnp.exp(jnp.sum(lam_q1.astype(F32) * lam_k1.astype(F32)))
           - jnp.exp(jnp.sum(lam_q2.astype(F32) * lam_k2.astype(F32))) + lambda_init)
    slopes = jnp.asarray([2.0 ** (-8.0 * (h + 1) / DA_HEADS) for h in range(DA_HEADS)], F32)
    scal = jnp.concatenate([slopes, lam.reshape(1), jnp.full((1,), 1.0 - lambda_init, F32)])
    dbias, pos = _da_tables(slopes)
    return dict(
        g1=ln1_g.reshape(1, D_MODEL), w_in=w_in.astype(BF16), gn=gn, gm=gm, scal=scal,
        subln=da_subln_g.reshape(LANES, 1), dbias=dbias, pos=pos, rpb=na_rpb,
        w_out=w_out.astype(BF16), g2=ln2_g.reshape(1, D_MODEL), wqt=peer_w_query.T.astype(BF16),
        sk=peer_sub_keys.reshape(2 * PEER_HEADS, PEER_KEYS, LANES).astype(BF16),
        ut=peer_u.T.astype(BF16), v=peer_v.astype(BF16))


def _encoder_layer(x, p, tables):
    B, S, _ = x.shape
    x2d = x.reshape(B * S, D_MODEL)
    qkv = _inproj(x2d, p["g1"], p["w_in"], p["gn"], p["gm"])
    oda = _diff_attention(qkv, p["scal"], p["subln"], p["dbias"], p["pos"], B, S)
    ona = _neighborhood_attention(qkv, tables, B, S)
    x1, xn, st = _mix(x2d, oda, ona, p["w_out"], p["g2"], p["wqt"], p["sk"])
    g3 = _route(st)
    y = _peer(x1, xn, p["ut"], p["v"], g3)
    return y.reshape(B, S, D_MODEL)


def kernel(x_prompt, x_sample, ln1_g, w_in, da_q_norm_g, da_k_norm_g, da_lambda_q1, da_lambda_k1,
           da_lambda_q2, da_lambda_k2, da_subln_g, na_q_norm_g, na_k_norm_g, na_rpb, w_out, ln2_g,
           peer_w_query, peer_sub_keys, peer_u, peer_v):
    hp, hs = x_prompt, x_sample
    for l in range(ln1_g.shape[0]):
        p = _prepare_params(l, ln1_g[l], w_in[l], da_q_norm_g[l], da_k_norm_g[l], da_lambda_q1[l],
                            da_lambda_k1[l], da_lambda_q2[l], da_lambda_k2[l], da_subln_g[l],
                            na_q_norm_g[l], na_k_norm_g[l], na_rpb[l], w_out[l], ln2_g[l],
                            peer_w_query[l], peer_sub_keys[l], peer_u[l], peer_v[l])
        outs = []
        for x in (hp, hs):
            tables = _na_bias_tables(p["rpb"], x.shape[1] // GRID_W)
            outs.append(_encoder_layer(x, p, tables))
        hp, hs = outs
    return (hp, hs)
```

```python
import functools
import math

import numpy as np
import jax
import jax.numpy as jnp
from jax import lax
from jax.experimental import pallas as pl
from jax.experimental.pallas import tpu as pltpu

F32 = jnp.float32
BF16 = jnp.bfloat16

D_MODEL = 1024
HEAD_DIM = 64
LANES = 128
MXU_N = 256
DA_HEADS = 4
DA_WIDTH = 512
NA_HEADS = 8
NA_WIDTH = 512
GRID_W = 64
NA_WIN_R = 8
NA_WIN_C = 16
IN_COLS = 3 * DA_WIDTH + 3 * NA_WIDTH
PEER_HEADS = 8
PEER_KEYS = 128
PEER_QDIM = 256
PEER_TOPK = 16
PEER_EXPERTS = PEER_KEYS * PEER_KEYS
RMS_EPS = 1e-6
NEG = -1e30
VMEM_LIMIT = 56 * 1024 * 1024

_DQ, _DK, _DV = 0, 4, 8
_NQ, _NK, _NV = 12, 16, 20


def _cparams(sem):
    return pltpu.CompilerParams(dimension_semantics=sem, vmem_limit_bytes=VMEM_LIMIT)


def _group_ms(x, gm):
    sq = x * x
    hi = sq.astype(BF16)
    lo = (sq - hi.astype(F32)).astype(BF16)
    return (jnp.dot(hi, gm, preferred_element_type=F32)
            + jnp.dot(lo, gm, preferred_element_type=F32))


def _inproj_kernel(x_ref, g1_ref, w_ref, gn_ref, gm_ref, o_ref):
    x = x_ref[...]
    ms = jnp.mean(x * x, axis=-1, keepdims=True)
    xn = (x * lax.rsqrt(ms + RMS_EPS) * g1_ref[...]).astype(BF16)
    proj = jnp.dot(xn, w_ref[...], preferred_element_type=F32)
    gm = gm_ref[...]
    for sec, row in ((0, 0), (1, 1), (3, 2), (4, 3)):
        xs = proj[:, sec * 512:(sec + 1) * 512]
        y = xs * lax.rsqrt(_group_ms(xs, gm) + RMS_EPS) * gn_ref[row:row + 1, :]
        o_ref[:, sec * 512:(sec + 1) * 512] = y.astype(BF16)
    for sec in (2, 5):
        o_ref[:, sec * 512:(sec + 1) * 512] = proj[:, sec * 512:(sec + 1) * 512].astype(BF16)


def _inproj(x2d, g1, w_in_bf, gn, gm, tm=512):
    T = x2d.shape[0]
    return pl.pallas_call(
        _inproj_kernel,
        grid=(T // tm,),
        in_specs=[
            pl.BlockSpec((tm, D_MODEL), lambda i: (i, 0)),
            pl.BlockSpec((1, D_MODEL), lambda i: (0, 0)),
            pl.BlockSpec((D_MODEL, IN_COLS), lambda i: (0, 0)),
            pl.BlockSpec((4, 512), lambda i: (0, 0)),
            pl.BlockSpec((512, 512), lambda i: (0, 0)),
        ],
        out_specs=pl.BlockSpec((tm, IN_COLS), lambda i: (i, 0)),
        out_shape=jax.ShapeDtypeStruct((T, IN_COLS), BF16),
        compiler_params=_cparams(("parallel",)),
        name="inproj",
    )(x2d, g1, w_in_bf, gn, gm)


DA_TQ = 512
DA_TK = 512
DA_CHUNK = MXU_N
DA_POS_LANES = 6
DA_VROWS = 144
LOG2E = math.log2(math.e)


def _da_kernel(sc_ref, q_ref, qn_ref, k_ref, v_ref, g_ref, db_ref, pos_ref, o_ref,
               qst_ref, qstn_ref, vt_ref, m_ref, acc_ref, sta_ref, stb_ref, *, nk):
    tq, tk = DA_TQ, DA_TK
    h = pl.program_id(1)
    qi = pl.program_id(2)
    nq = pl.num_programs(2)
    slope = sc_ref[h]
    lam = sc_ref[4]
    post = sc_ref[5]
    unroll = nk if nk <= 16 else (8 if nk % 8 == 0 else 2)
    handoff = unroll == nk

    @pl.when(qi == 0)
    def _transpose_values():
        ones_row = jnp.where(lax.broadcasted_iota(jnp.int32, (DA_VROWS - LANES, tk), 0) == 0, 1.0, 0.0)
        for c in range(nk):
            vt_ref[c, 0:LANES, :] = v_ref[c * tk:(c + 1) * tk, :].astype(F32).T.astype(BF16)
            vt_ref[c, LANES:DA_VROWS, :] = ones_row.astype(BF16)

    def query_operand(dst_ref, src_ref):
        qT = src_ref[...].astype(F32).T
        row = lax.broadcasted_iota(jnp.int32, (LANES, tq), 0)
        dst_ref[0:LANES, 0:tq] = jnp.where(row < HEAD_DIM, qT, 0.0).astype(BF16)
        dst_ref[0:LANES, tq:2 * tq] = jnp.where(row >= HEAD_DIM, qT, 0.0).astype(BF16)
        row2 = lax.broadcasted_iota(jnp.int32, (LANES, 2 * tq), 0)
        dst_ref[LANES:2 * LANES, :] = jnp.where(row2 < DA_POS_LANES, 1.0, 0.0).astype(BF16)

    query_operand(qst_ref, q_ref)
    if handoff:
        query_operand(qstn_ref, qn_ref)
    m_ref[...] = jnp.full((1, 2 * tq), NEG, F32)
    acc_ref[...] = jnp.zeros((DA_VROWS, 2 * tq), F32)

    pos = pos_ref[0].astype(F32)
    col = lax.broadcasted_iota(jnp.int32, (1, 2 * tq), 1)
    icol = jnp.where(col >= tq, col - tq, col).astype(F32)

    nch = 2 * tq // DA_CHUNK

    def cols(n):
        return slice(n * DA_CHUNK, (n + 1) * DA_CHUNK)

    def side(blk):
        return jnp.where(qi > blk, 1.0, jnp.where(qi < blk, -1.0, 0.0))

    def key_operand(blk, sgn):
        k0 = pl.multiple_of(blk * tk, tk)
        return jnp.concatenate([k_ref[pl.ds(k0, tk), :], (pos * sgn).astype(BF16)], axis=1)

    def softmax_pv(n, src_ref, vtb, cq):
        sl = cols(n)
        st = src_ref[:, sl]
        c_n = cq[:, sl]
        m_old = m_ref[:, sl]
        m_new = jnp.maximum(m_old, jnp.max(st, axis=0, keepdims=True) + c_n)
        p = jnp.exp2(st - (m_new - c_n)).astype(BF16)
        alpha = jnp.exp2(m_old - m_new)
        acc_ref[:, sl] = alpha * acc_ref[:, sl] + jnp.dot(vtb, p, preferred_element_type=F32)
        m_ref[:, sl] = m_new

    def diagonal_scores(n, ka, qop_ref):
        return jnp.dot(ka, qop_ref[:, cols(n)], preferred_element_type=F32) + db_ref[0, :, cols(n)]

    def step(cur, nxt, src_ref, dst_ref, next_query=False):
        ka = key_operand(nxt, 0.0 if next_query else side(nxt))
        vtb = vt_ref[cur]
        cq = (-side(cur) * slope * LOG2E) * ((qi * tq - cur * tk).astype(F32) + icol)

        def scores(n):
            if next_query:
                dst_ref[:, cols(n)] = diagonal_scores(n, ka, qstn_ref)
            else:
                dst_ref[:, cols(n)] = jnp.dot(ka, qst_ref[:, cols(n)], preferred_element_type=F32)

        scores(0)
        scores(1)
        for n in range(nch):
            softmax_pv(n, src_ref, vtb, cq)
            if n + 2 < nch:
                scores(n + 2)

    def other(x):
        return x + (x >= qi).astype(jnp.int32)

    def first_scores():
        ka0 = key_operand(qi, 0.0)
        for n in range(nch):
            sta_ref[:, cols(n)] = diagonal_scores(n, ka0, qst_ref)

    if handoff:
        pl.when(qi == 0)(first_scores)
        cur = qi
        for s in range(nk):
            src_ref, dst_ref = (sta_ref, stb_ref) if s % 2 == 0 else (stb_ref, sta_ref)
            if s < nk - 1:
                nxt = other(s)
                step(cur, nxt, src_ref, dst_ref)
                cur = nxt
            else:
                step(cur, jnp.minimum(qi + 1, nq - 1), src_ref, dst_ref, next_query=True)
    else:
        first_scores()

        def trip(ii, carry):
            cur = jnp.where(ii == 0, qi, other(unroll * ii - 1))
            for s in range(unroll):
                nxt = other(jnp.minimum(unroll * ii + s, nk - 2))
                src_ref, dst_ref = (sta_ref, stb_ref) if s % 2 == 0 else (stb_ref, sta_ref)
                step(cur, nxt, src_ref, dst_ref)
                cur = nxt
            return carry

        lax.fori_loop(0, nk // unroll, trip, 0)

    on = acc_ref[0:LANES, :] * (1.0 / acc_ref[LANES:LANES + 1, :])
    oT = on[:, 0:tq] - lam * on[:, tq:2 * tq]
    ms = jnp.mean(oT * oT, axis=0, keepdims=True)
    oT = oT * lax.rsqrt(ms + RMS_EPS) * g_ref[...] * post
    o_ref[...] = oT.T.astype(BF16)


def _da_tables(slopes):
    jj = np.arange(DA_TK)[:, None]
    ii = np.concatenate([np.arange(DA_TQ), np.arange(DA_TQ)])[None, :]
    dist = jnp.asarray(np.abs(ii - jj).astype(np.float32))
    dbias = -(slopes * LOG2E)[:, None, None] * dist[None]
    j = np.arange(DA_TK)
    terms = []
    for part in ((j >> 5) * 32.0, (j & 31) * 1.0):
        rem = (slopes * LOG2E)[:, None] * jnp.asarray(part.astype(np.float32))[None, :]
        for _ in range(3):
            piece = rem.astype(BF16)
            terms.append(piece)
            rem = rem - piece.astype(F32)
    pos = jnp.stack(terms, axis=-1)
    pos = jnp.concatenate([pos, jnp.zeros(pos.shape[:2] + (LANES - DA_POS_LANES,), BF16)], axis=-1)
    return dbias, pos


def _diff_attention(qkv, scal, subln_col, dbias, pos, B, S):
    T = B * S
    tq, tk = DA_TQ, DA_TK
    nq, nk = S // tq, S // tk
    kern = functools.partial(_da_kernel, nk=nk)
    return pl.pallas_call(
        kern,
        grid=(B, DA_HEADS, nq),
        in_specs=[
            pl.BlockSpec(memory_space=pltpu.SMEM),
            pl.BlockSpec((tq, LANES), lambda b, h, i: (b * nq + i, _DQ + h)),
            pl.BlockSpec((tq, LANES), lambda b, h, i: (b * nq + jnp.minimum(i + 1, nq - 1), _DQ + h)),
            pl.BlockSpec((S, LANES), lambda b, h, i: (b, _DK + h)),
            pl.BlockSpec((S, LANES), lambda b, h, i: (b, _DV + h)),
            pl.BlockSpec((LANES, 1), lambda b, h, i: (0, 0)),
            pl.BlockSpec((1, tk, 2 * tq), lambda b, h, i: (h, 0, 0)),
            pl.BlockSpec((1, tk, LANES), lambda b, h, i: (h, 0, 0)),
        ],
        out_specs=pl.BlockSpec((tq, LANES), lambda b, h, i: (b * nq + i, h)),
        out_shape=jax.ShapeDtypeStruct((T, DA_WIDTH), BF16),
        scratch_shapes=[
            pltpu.VMEM((2 * LANES, 2 * tq), BF16),
            pltpu.VMEM((2 * LANES, 2 * tq), BF16),
            pltpu.VMEM((nk, DA_VROWS, tk), BF16),
            pltpu.VMEM((1, 2 * tq), F32),
            pltpu.VMEM((DA_VROWS, 2 * tq), F32),
            pltpu.VMEM((tk, 2 * tq), F32),
            pltpu.VMEM((tk, 2 * tq), F32),
        ],
        compiler_params=_cparams(("parallel", "parallel", "arbitrary")),
        name="diff_attn",
    )(scal, qkv, qkv, qkv, qkv, subln_col, dbias, pos)


NA_QROWS = 8
NA_KROWS = 16
NA_TQ = NA_QROWS * GRID_W
NA_TK = NA_KROWS * GRID_W


def _na_bias_tables(rpb, rows):
    cols = np.arange(GRID_W)
    cstart = np.clip(cols - NA_WIN_C // 2, 0, GRID_W - NA_WIN_C)
    kc = cols[None, :]
    col_ok = (kc >= cstart[:, None]) & (kc < cstart[:, None] + NA_WIN_C)
    dc = np.clip(kc - cols[:, None] + (NA_WIN_C - 1), 0, 2 * NA_WIN_C - 2)
    t = jnp.where(jnp.asarray(col_ok)[None, None], rpb[:, :, dc] * LOG2E, NEG)
    t = jnp.concatenate([t, jnp.full((NA_HEADS, 1, GRID_W, GRID_W), NEG, F32)], axis=1)
    tables = []
    for r0, w0 in ((0, 0), (rows // 2 // NA_QROWS * NA_QROWS, None), (rows - NA_QROWS, rows - NA_KROWS)):
        if w0 is None:
            r0 = max(NA_QROWS, min(r0, rows - 2 * NA_QROWS))
            w0 = r0 - NA_WIN_R // 2
        r = r0 + np.arange(NA_QROWS)[:, None]
        kr = w0 + np.arange(NA_KROWS)[None, :]
        rs = np.clip(r - NA_WIN_R // 2, 0, rows - NA_WIN_R)
        ok = (kr >= rs) & (kr < rs + NA_WIN_R)
        dr = np.where(ok, kr - r + (NA_WIN_R - 1), 2 * NA_WIN_R - 1)
        tb = t[:, dr]
        tables.append(jnp.transpose(tb, (0, 1, 3, 2, 4)).reshape(NA_HEADS, NA_TQ, NA_TK))
    return jnp.stack(tables)


def _na_kernel(q_ref, k_ref, v_ref, b_ref, o_ref, *, rows):
    j = pl.program_id(2)
    w0 = jnp.clip(j * NA_QROWS - NA_WIN_R // 2, 0, rows - NA_KROWS)
    start = pl.multiple_of(w0 * GRID_W, GRID_W)
    kw = k_ref[pl.ds(start, NA_TK), :]
    vw = v_ref[pl.ds(start, NA_TK), :]
    half = NA_TQ // 2
    lane = lax.broadcasted_iota(jnp.int32, (half, LANES), 1)
    chains = [(qh, hh) for qh in range(2) for hh in range(2)]

    def scores(qh, hh):
        q = q_ref[qh * half:(qh + 1) * half, :]
        msk = (lane < HEAD_DIM) if hh == 0 else (lane >= HEAD_DIM)
        qm = jnp.where(msk, q, jnp.zeros_like(q))
        s = lax.dot_general(qm, kw, (((1,), (1,)), ((), ())), preferred_element_type=F32)
        return s + b_ref[0, hh, qh * half:(qh + 1) * half, :]

    def attend(s):
        m = jnp.max(s, axis=-1, keepdims=True)
        p = jnp.exp2(s - m)
        l = jnp.sum(p, axis=-1, keepdims=True)
        return jnp.dot(p.astype(BF16), vw, preferred_element_type=F32) / l

    outs = {}
    pending = scores(*chains[0])
    for c, chain in enumerate(chains):
        s = pending
        if c + 1 < len(chains):
            pending = scores(*chains[c + 1])
        outs[chain] = attend(s)
    for qh in range(2):
        o_ref[qh * half:(qh + 1) * half, :] = jnp.where(
            lane < HEAD_DIM, outs[(qh, 0)], outs[(qh, 1)]).astype(BF16)


def _neighborhood_attention(qkv, tables, B, S):
    T = B * S
    rows = S // GRID_W
    nblk = rows // NA_QROWS
    kern = functools.partial(_na_kernel, rows=rows)

    def btype(j):
        return jnp.where(j == 0, 0, jnp.where(j == nblk - 1, 2, 1))

    return pl.pallas_call(
        kern,
        grid=(NA_HEADS // 2, B, nblk),
        in_specs=[
            pl.BlockSpec((NA_TQ, LANES), lambda p, b, j: (b * nblk + j, _NQ + p)),
            pl.BlockSpec((S, LANES), lambda p, b, j: (b, _NK + p)),
            pl.BlockSpec((S, LANES), lambda p, b, j: (b, _NV + p)),
            pl.BlockSpec((1, 2, NA_TQ, NA_TK), lambda p, b, j: (btype(j), p, 0, 0)),
        ],
        out_specs=pl.BlockSpec((NA_TQ, LANES), lambda p, b, j: (b * nblk + j, p)),
        out_shape=jax.ShapeDtypeStruct((T, NA_WIDTH), BF16),
        compiler_params=_cparams(("parallel", "parallel", "parallel")),
        name="nbr_attn",
    )(qkv, qkv, qkv, tables)


def _mix_kernel(x_ref, oda_ref, ona_ref, wo_ref, g2_ref, wqt_ref, sk_ref, x1_ref, xn_ref, st_ref):
    x1 = (x_ref[...]
          + jnp.dot(oda_ref[...], wo_ref[0:DA_WIDTH, :], preferred_element_type=F32)
          + jnp.dot(ona_ref[...], wo_ref[DA_WIDTH:, :], preferred_element_type=F32))
    x1_ref[...] = x1
    ms = jnp.mean(x1 * x1, axis=-1, keepdims=True)
    xn = (x1 * lax.rsqrt(ms + RMS_EPS) * g2_ref[...]).astype(BF16)
    xn_ref[...] = xn
    qt = lax.dot_general(wqt_ref[...], xn, (((1,), (1,)), ((), ())), preferred_element_type=F32).astype(BF16)
    for hp in range(2 * PEER_HEADS):
        st_ref[hp * PEER_KEYS:(hp + 1) * PEER_KEYS, :] = jnp.dot(
            sk_ref[hp], qt[hp * LANES:(hp + 1) * LANES, :], preferred_element_type=F32)


def _mix(x2d, oda, ona, wo_bf, g2, wqt_bf, sk_bf, tm=512):
    T = x2d.shape[0]
    nsc = 2 * PEER_HEADS * PEER_KEYS
    return pl.pallas_call(
        _mix_kernel,
        grid=(T // tm,),
        in_specs=[
            pl.BlockSpec((tm, D_MODEL), lambda i: (i, 0)),
            pl.BlockSpec((tm, DA_WIDTH), lambda i: (i, 0)),
            pl.BlockSpec((tm, NA_WIDTH), lambda i: (i, 0)),
            pl.BlockSpec((D_MODEL, D_MODEL), lambda i: (0, 0)),
            pl.BlockSpec((1, D_MODEL), lambda i: (0, 0)),
            pl.BlockSpec((nsc, D_MODEL), lambda i: (0, 0)),
            pl.BlockSpec((2 * PEER_HEADS, PEER_KEYS, LANES), lambda i: (0, 0, 0)),
        ],
        out_specs=[
            pl.BlockSpec((tm, D_MODEL), lambda i: (i, 0)),
            pl.BlockSpec((tm, D_MODEL), lambda i: (i, 0)),
            pl.BlockSpec((nsc, tm), lambda i: (0, i)),
        ],
        out_shape=[
            jax.ShapeDtypeStruct((T, D_MODEL), F32),
            jax.ShapeDtypeStruct((T, D_MODEL), BF16),
            jax.ShapeDtypeStruct((nsc, T), F32),
        ],
        compiler_params=_cparams(("parallel",)),
        name="mix",
    )(x2d, oda, ona, wo_bf, g2, wqt_bf, sk_bf)


def _sort16_network():
    n, pairs, p = 16, [], 1
    while p < n:
        k = p
        while k >= 1:
            j = k % p
            while j <= n - 1 - k:
                for i in range(min(k - 1, n - j - k - 1) + 1):
                    if (i + j) // (2 * p) == (i + j + k) // (2 * p):
                        pairs.append((i + j, i + j + k))
                j += 2 * k
            k //= 2
        p *= 2
    return pairs


def _top16_tiles(val, idx):
    nt = len(val)
    L = val[0].shape[1]
    val, idx = list(val), list(idx)
    for a, b in [(a, b) for a, b in _sort16_network() if b < nt]:
        swap = (val[b] > val[a]) | ((val[b] == val[a]) & (idx[b] < idx[a]))
        val[a], val[b] = jnp.maximum(val[a], val[b]), jnp.minimum(val[a], val[b])
        idx[a], idx[b] = jnp.where(swap, idx[b], idx[a]), jnp.where(swap, idx[a], idx[b])
    slot = lax.broadcasted_iota(jnp.int32, (PEER_TOPK, L), 0)
    vals = jnp.zeros((PEER_TOPK, L), F32)
    idxs = jnp.zeros((PEER_TOPK, L), F32)
    for it in range(PEER_TOPK):
        m = jnp.max(val[0], axis=0, keepdims=True)
        ix = jnp.min(jnp.where(val[0] == m, idx[0], 1e9), axis=0, keepdims=True)
        vals = jnp.where(slot == it, m, vals)
        idxs = jnp.where(slot == it, ix, idxs)
        win = idx[0] == ix
        live = PEER_TOPK - 1 - it
        for k in range(min(nt - 1, live)):
            val[k] = jnp.where(win, val[k + 1], val[k])
            idx[k] = jnp.where(win, idx[k + 1], idx[k])
        if nt - 1 < live:
            val[nt - 1] = jnp.where(win, -jnp.inf, val[nt - 1])
    return vals, idxs


def _top16_rows(s):
    L = s.shape[1]
    sub = lax.broadcasted_iota(jnp.int32, (8, L), 0).astype(F32)
    ntile = s.shape[0] // 8
    return _top16_tiles([s[8 * v:8 * v + 8, :] for v in range(ntile)],
                        [sub + float(8 * v) for v in range(ntile)])


def _route_topk(st_ref, per_head):
    L = st_ref.shape[1]
    ncand = 72
    r = lax.broadcasted_iota(jnp.int32, (ncand, L), 0)
    dup = ((r >= 48) & (r < 53)) | ((r >= 64) & (r < 69))
    pos = jnp.where(r < 16, r,
                    jnp.where(r < 48, (1 + ((r - 16) >> 3)) * PEER_TOPK + ((r - 16) & 7),
                              jnp.where(r < 64, (r - 48) * PEER_TOPK, (r - 64) * PEER_TOPK + 1)))
    pos = jnp.where(dup, 1000 + r, pos).astype(F32)
    pos_tiles = [pos[8 * v:8 * v + 8, :] for v in range(ncand // 8)]
    slot = lax.broadcasted_iota(jnp.int32, (PEER_TOPK, L), 0)
    slotf = slot.astype(F32)
    his, los, gates = [], [], []
    for h in range(PEER_HEADS):
        s1, i1 = _top16_rows(st_ref[(2 * h) * PEER_KEYS:(2 * h + 1) * PEER_KEYS, :])
        s2, i2 = _top16_rows(st_ref[(2 * h + 1) * PEER_KEYS:(2 * h + 2) * PEER_KEYS, :])
        cand = [s1[0:1, :] + s2[0:8, :], s1[0:1, :] + s2[8:16, :]]
        for a in range(1, 5):
            cand.append(s1[a:a + 1, :] + s2[0:8, :])
        cand += [s1[0:8, :] + s2[0:1, :], s1[8:16, :] + s2[0:1, :], s1[0:8, :] + s2[1:2, :]]
        cand = [jnp.where(dup[8 * v:8 * v + 8, :], -jnp.inf, c) if v in (6, 8) else c
                for v, c in enumerate(cand)]
        top, tpos = _top16_tiles(cand, pos_tiles)
        ta = jnp.floor(tpos * (1.0 / PEER_TOPK))
        tb = tpos - ta * PEER_TOPK
        thi = jnp.zeros((PEER_TOPK, L), F32)
        tlo = jnp.zeros((PEER_TOPK, L), F32)
        for a in range(PEER_TOPK):
            thi = jnp.where(ta == float(a), i1[a:a + 1, :], thi)
            tlo = jnp.where(tb == float(a), i2[a:a + 1, :], tlo)
        e = jnp.exp(top - top[0:1, :])
        his.append(thi)
        los.append(tlo)
        gates.append(e / jnp.sum(e, axis=0, keepdims=True))
        per_head(h)
    return (jnp.concatenate(his, axis=0).T, jnp.concatenate(los, axis=0).T,
            jnp.concatenate(gates, axis=0).T)


GATE_ROWS = 8
ROUTE_L = 128


def _route_gates(hi_ref, lo_ref, g_ref, o_ref, t_lo, t_hi):
    nsel = hi_ref.shape[1]
    rid_b = lax.broadcasted_iota(jnp.int32, (PEER_KEYS, nsel), 0).astype(F32).astype(BF16)
    one_b = jnp.ones((PEER_KEYS, nsel), BF16)
    zero_b = jnp.zeros((PEER_KEYS, nsel), BF16)
    for t in range(t_lo, t_hi):
        hi_b = jnp.broadcast_to(hi_ref[t:t + 1, :].astype(BF16), (PEER_KEYS, nsel))
        lo_b = jnp.broadcast_to(lo_ref[t:t + 1, :].astype(BF16), (PEER_KEYS, nsel))
        g_b = jnp.broadcast_to(g_ref[t:t + 1, :].astype(BF16), (PEER_KEYS, nsel))
        qt = jnp.where(rid_b == hi_b, one_b, zero_b)
        pt = jnp.where(rid_b == lo_b, g_b, zero_b)
        gt = lax.dot_general(qt, pt, (((1,), (1,)), ((), ())), preferred_element_type=F32)
        for jb in range(PEER_EXPERTS // PEER_EBLK):
            o_ref[jb, t] = gt[jb * GATE_ROWS:(jb + 1) * GATE_ROWS, :]


def _route_kernel(st_ref, o_ref, hi_s, lo_s, g_s):
    @pl.when(pl.program_id(0) == 0)
    def _no_previous_block():
        hi_s[...] = jnp.zeros_like(hi_s)
        lo_s[...] = jnp.zeros_like(lo_s)
        g_s[...] = jnp.zeros_like(g_s)

    per = hi_s.shape[0] // PEER_HEADS

    def gates_slice(h):
        _route_gates(hi_s, lo_s, g_s, o_ref, h * per, (h + 1) * per)

    hi, lo, g = _route_topk(st_ref, gates_slice)
    hi_s[...] = hi
    lo_s[...] = lo
    g_s[...] = g


def _route(st):
    nsc, T = st.shape
    L = ROUTE_L
    nblk = T // L
    nsel = PEER_HEADS * PEER_TOPK
    njb = PEER_EXPERTS // PEER_EBLK
    return pl.pallas_call(
        _route_kernel,
        grid=(nblk + 1,),
        in_specs=[pl.BlockSpec((nsc, L), lambda i: (0, jnp.minimum(i, nblk - 1)))],
        out_specs=pl.BlockSpec((njb, L, GATE_ROWS, PEER_KEYS), lambda i: (0, jnp.maximum(i - 1, 0), 0, 0)),
        out_shape=jax.ShapeDtypeStruct((njb, T, GATE_ROWS, PEER_KEYS), F32),
        scratch_shapes=[pltpu.VMEM((L, nsel), F32)] * 3,
        compiler_params=_cparams(("arbitrary",)),
        name="peer_route",
    )(st)


PEER_TM = 1024
PEER_EBLK = 1024
assert GATE_ROWS * PEER_KEYS == PEER_EBLK


def _gelu_tanh(x):
    return 0.5 * x * (1.0 + jnp.tanh(math.sqrt(2.0 / math.pi) * (x + 0.044715 * (x * x * x))))


def _peer_kernel(x1_ref, xn_ref, ut_ref, v_ref, g3_ref, y_ref):
    j = pl.program_id(1)

    @pl.when(j == 0)
    def _residual():
        y_ref[...] = x1_ref[...]

    hmat = jnp.dot(xn_ref[...], ut_ref[...], preferred_element_type=F32)
    tm = xn_ref.shape[0]
    gates = jnp.concatenate([g3_ref[pl.ds(c, tm, stride=GATE_ROWS), :] for c in range(GATE_ROWS)], axis=1)
    w = (gates * _gelu_tanh(hmat)).astype(BF16)
    y_ref[...] += jnp.dot(w, v_ref[...], preferred_element_type=F32)


def _peer(x1, xn, ut_bf, v_bf, g3):
    T = x1.shape[0]
    tm = PEER_TM
    return pl.pallas_call(
        _peer_kernel,
        grid=(T // tm, PEER_EXPERTS // PEER_EBLK),
        in_specs=[
            pl.BlockSpec((tm, D_MODEL), lambda i, j: (i, 0)),
            pl.BlockSpec((tm, D_MODEL), lambda i, j: (i, 0)),
            pl.BlockSpec((D_MODEL, PEER_EBLK), lambda i, j: (0, j)),
            pl.BlockSpec((PEER_EBLK, D_MODEL), lambda i, j: (j, 0)),
            pl.BlockSpec((tm * GATE_ROWS, PEER_KEYS), lambda i, j: (j * (T // tm) + i, 0)),
        ],
        out_specs=pl.BlockSpec((tm, D_MODEL), lambda i, j: (i, 0)),
        out_shape=jax.ShapeDtypeStruct((T, D_MODEL), F32),
        compiler_params=_cparams(("parallel", "arbitrary")),
        name="peer_mlp",
    )(x1, xn, ut_bf, v_bf, g3.reshape(-1, PEER_KEYS))


def _prepare_params(layer_idx, ln1_g, w_in, da_q_norm_g, da_k_norm_g, lam_q1, lam_k1, lam_q2, lam_k2,
                    da_subln_g, na_q_norm_g, na_k_norm_g, na_rpb, w_out, ln2_g,
                    peer_w_query, peer_sub_keys, peer_u, peer_v):
    scale = HEAD_DIM ** -0.5
    gn = jnp.stack([jnp.tile(da_q_norm_g, 8) * (scale * LOG2E), jnp.tile(da_k_norm_g, 8),
                    jnp.tile(na_q_norm_g, 8) * (scale * LOG2E), jnp.tile(na_k_norm_g, 8)]).astype(F32)
    grp = np.arange(512) // HEAD_DIM
    gm = jnp.asarray((grp[:, None] == grp[None, :]).astype(np.float32) / HEAD_DIM, BF16)
    lambda_init = 0.8 - 0.6 * math.exp(-0.3 * layer_idx)
    lam = (jnp.exp(jnp.sum(lam_q1.astype(F32) * lam_k1.astype(F32)))
           - jnp.exp(jnp.sum(lam_q2.astype(F32) * lam_k2.astype(F32))) + lambda_init)
    slopes = jnp.asarray([2.0 ** (-8.0 * (h + 1) / DA_HEADS) for h in range(DA_HEADS)], F32)
    scal = jnp.concatenate([slopes, lam.reshape(1), jnp.full((1,), 1.0 - lambda_init, F32)])
    dbias, pos = _da_tables(slopes)
    return dict(
        g1=ln1_g.reshape(1, D_MODEL), w_in=w_in.astype(BF16), gn=gn, gm=gm, scal=scal,
        subln=da_subln_g.reshape(LANES, 1), dbias=dbias, pos=pos, rpb=na_rpb,
        w_out=w_out.astype(BF16), g2=ln2_g.reshape(1, D_MODEL), wqt=peer_w_query.T.astype(BF16),
        sk=peer_sub_keys.reshape(2 * PEER_HEADS, PEER_KEYS, LANES).astype(BF16),
        ut=peer_u.T.astype(BF16), v=peer_v.astype(BF16))


def _encoder_layer(x, p, tables):
    B, S, _ = x.shape
    x2d = x.reshape(B * S, D_MODEL)
    qkv = _inproj(x2d, p["g1"], p["w_in"], p["gn"], p["gm"])
    oda = _diff_attention(qkv, p["scal"], p["subln"], p["dbias"], p["pos"], B, S)
    ona = _neighborhood_attention(qkv, tables, B, S)
    x1, xn, st = _mix(x2d, oda, ona, p["w_out"], p["g2"], p["wqt"], p["sk"])
    g3 = _route(st)
    y = _peer(x1, xn, p["ut"], p["v"], g3)
    return y.reshape(B, S, D_MODEL)


def kernel(x_prompt, x_sample, ln1_g, w_in, da_q_norm_g, da_k_norm_g, da_lambda_q1, da_lambda_k1,
           da_lambda_q2, da_lambda_k2, da_subln_g, na_q_norm_g, na_k_norm_g, na_rpb, w_out, ln2_g,
           peer_w_query, peer_sub_keys, peer_u, peer_v):
    hp, hs = x_prompt, x_sample
    for l in range(ln1_g.shape[0]):
        p = _prepare_params(l, ln1_g[l], w_in[l], da_q_norm_g[l], da_k_norm_g[l], da_lambda_q1[l],
                            da_lambda_k1[l], da_lambda_q2[l], da_lambda_k2[l], da_subln_g[l],
                            na_q_norm_g[l], na_k_norm_g[l], na_rpb[l], w_out[l], ln2_g[l],
                            peer_w_query[l], peer_sub_keys[l], peer_u[l], peer_v[l])
        outs = []
        for x in (hp, hs):
            tables = _na_bias_tables(p["rpb"], x.shape[1] // GRID_W)
            outs.append(_encoder_layer(x, p, tables))
        hp, hs = outs
    return (hp, hs)
```

```python
import functools
import math

import numpy as np
import jax
import jax.numpy as jnp
from jax import lax
from jax.experimental import pallas as pl
from jax.experimental.pallas import tpu as pltpu

F32 = jnp.float32
BF16 = jnp.bfloat16

D_MODEL = 1024
HEAD_DIM = 64
LANES = 128
MXU_N = 256
DA_HEADS = 4
DA_WIDTH = 512
NA_HEADS = 8
NA_WIDTH = 512
GRID_W = 64
NA_WIN_R = 8
NA_WIN_C = 16
IN_COLS = 3 * DA_WIDTH + 3 * NA_WIDTH
PEER_HEADS = 8
PEER_KEYS = 128
PEER_QDIM = 256
PEER_TOPK = 16
PEER_EXPERTS = PEER_KEYS * PEER_KEYS
RMS_EPS = 1e-6
NEG = -1e30
VMEM_LIMIT = 56 * 1024 * 1024

_DQ, _DK, _DV = 0, 4, 8
_NQ, _NK, _NV = 12, 16, 20


def _cparams(sem):
    return pltpu.CompilerParams(dimension_semantics=sem, vmem_limit_bytes=VMEM_LIMIT)


def _group_ms(x, gm):
    sq = x * x
    hi = sq.astype(BF16)
    lo = (sq - hi.astype(F32)).astype(BF16)
    return (jnp.dot(hi, gm, preferred_element_type=F32)
            + jnp.dot(lo, gm, preferred_element_type=F32))


def _inproj_kernel(x_ref, g1_ref, w_ref, gn_ref, gm_ref, o_ref):
    x = x_ref[...]
    ms = jnp.mean(x * x, axis=-1, keepdims=True)
    xn = (x * lax.rsqrt(ms + RMS_EPS) * g1_ref[...]).astype(BF16)
    proj = jnp.dot(xn, w_ref[...], preferred_element_type=F32)
    gm = gm_ref[...]
    for sec, row in ((0, 0), (1, 1), (3, 2), (4, 3)):
        xs = proj[:, sec * 512:(sec + 1) * 512]
        y = xs * lax.rsqrt(_group_ms(xs, gm) + RMS_EPS) * gn_ref[row:row + 1, :]
        o_ref[:, sec * 512:(sec + 1) * 512] = y.astype(BF16)
    for sec in (2, 5):
        o_ref[:, sec * 512:(sec + 1) * 512] = proj[:, sec * 512:(sec + 1) * 512].astype(BF16)


def _inproj(x2d, g1, w_in_bf, gn, gm, tm=512):
    T = x2d.shape[0]
    return pl.pallas_call(
        _inproj_kernel,
        grid=(T // tm,),
        in_specs=[
            pl.BlockSpec((tm, D_MODEL), lambda i: (i, 0)),
            pl.BlockSpec((1, D_MODEL), lambda i: (0, 0)),
            pl.BlockSpec((D_MODEL, IN_COLS), lambda i: (0, 0)),
            pl.BlockSpec((4, 512), lambda i: (0, 0)),
            pl.BlockSpec((512, 512), lambda i: (0, 0)),
        ],
        out_specs=pl.BlockSpec((tm, IN_COLS), lambda i: (i, 0)),
        out_shape=jax.ShapeDtypeStruct((T, IN_COLS), BF16),
        compiler_params=_cparams(("parallel",)),
        name="inproj",
    )(x2d, g1, w_in_bf, gn, gm)


DA_TQ = 512
DA_TK = 512
DA_CHUNK = MXU_N
DA_POS_LANES = 6
DA_VROWS = 144
LOG2E = math.log2(math.e)


def _da_kernel(sc_ref, q_ref, qn_ref, k_ref, v_ref, g_ref, db_ref, pos_ref, o_ref,
               qst_ref, qstn_ref, vt_ref, m_ref, acc_ref, sta_ref, stb_ref, *, nk):
    tq, tk = DA_TQ, DA_TK
    h = pl.program_id(1)
    qi = pl.program_id(2)
    nq = pl.num_programs(2)
    slope = sc_ref[h]
    lam = sc_ref[4]
    post = sc_ref[5]
    unroll = nk if nk <= 16 else (8 if nk % 8 == 0 else 2)
    handoff = unroll == nk

    @pl.when(qi == 0)
    def _transpose_values():
        ones_row = jnp.where(lax.broadcasted_iota(jnp.int32, (DA_VROWS - LANES, tk), 0) == 0, 1.0, 0.0)
        for c in range(nk):
            vt_ref[c, 0:LANES, :] = v_ref[c * tk:(c + 1) * tk, :].astype(F32).T.astype(BF16)
            vt_ref[c, LANES:DA_VROWS, :] = ones_row.astype(BF16)

    def query_operand(dst_ref, src_ref):
        qT = src_ref[...].astype(F32).T
        row = lax.broadcasted_iota(jnp.int32, (LANES, tq), 0)
        dst_ref[0:LANES, 0:tq] = jnp.where(row < HEAD_DIM, qT, 0.0).astype(BF16)
        dst_ref[0:LANES, tq:2 * tq] = jnp.where(row >= HEAD_DIM, qT, 0.0).astype(BF16)
        row2 = lax.broadcasted_iota(jnp.int32, (LANES, 2 * tq), 0)
        dst_ref[LANES:2 * LANES, :] = jnp.where(row2 < DA_POS_LANES, 1.0, 0.0).astype(BF16)

    query_operand(qst_ref, q_ref)
    if handoff:
        query_operand(qstn_ref, qn_ref)
    m_ref[...] = jnp.full((1, 2 * tq), NEG, F32)
    acc_ref[...] = jnp.zeros((DA_VROWS, 2 * tq), F32)

    pos = pos_ref[0].astype(F32)
    col = lax.broadcasted_iota(jnp.int32, (1, 2 * tq), 1)
    icol = jnp.where(col >= tq, col - tq, col).astype(F32)

    nch = 2 * tq // DA_CHUNK

    def cols(n):
        return slice(n * DA_CHUNK, (n + 1) * DA_CHUNK)

    def side(blk):
        return jnp.where(qi > blk, 1.0, jnp.where(qi < blk, -1.0, 0.0))

    def key_operand(blk, sgn):
        k0 = pl.multiple_of(blk * tk, tk)
        return jnp.concatenate([k_ref[pl.ds(k0, tk), :], (pos * sgn).astype(BF16)], axis=1)

    def softmax_pv(n, src_ref, vtb, cq):
        sl = cols(n)
        st = src_ref[:, sl]
        c_n = cq[:, sl]
        m_old = m_ref[:, sl]
        m_new = jnp.maximum(m_old, jnp.max(st, axis=0, keepdims=True) + c_n)
        p = jnp.exp2(st - (m_new - c_n)).astype(BF16)
        alpha = jnp.exp2(m_old - m_new)
        acc_ref[:, sl] = alpha * acc_ref[:, sl] + jnp.dot(vtb, p, preferred_element_type=F32)
        m_ref[:, sl] = m_new

    def diagonal_scores(n, ka, qop_ref):
        return jnp.dot(ka, qop_ref[:, cols(n)], preferred_element_type=F32) + db_ref[0, :, cols(n)]

    def step(cur, nxt, src_ref, dst_ref, next_query=False):
        ka = key_operand(nxt, 0.0 if next_query else side(nxt))
        vtb = vt_ref[cur]
        cq = (-side(cur) * slope * LOG2E) * ((qi * tq - cur * tk).astype(F32) + icol)

        def scores(n):
            if next_query:
                dst_ref[:, cols(n)] = diagonal_scores(n, ka, qstn_ref)
            else:
                dst_ref[:, cols(n)] = jnp.dot(ka, qst_ref[:, cols(n)], preferred_element_type=F32)

        scores(0)
        scores(1)
        for n in range(nch):
            softmax_pv(n, src_ref, vtb, cq)
            if n + 2 < nch:
                scores(n + 2)

    def other(x):
        return x + (x >= qi).astype(jnp.int32)

    def first_scores():
        ka0 = key_operand(qi, 0.0)
        for n in range(nch):
            sta_ref[:, cols(n)] = diagonal_scores(n, ka0, qst_ref)

    if handoff:
        pl.when(qi == 0)(first_scores)
        cur = qi
        for s in range(nk):
            src_ref, dst_ref = (sta_ref, stb_ref) if s % 2 == 0 else (stb_ref, sta_ref)
            if s < nk - 1:
                nxt = other(s)
                step(cur, nxt, src_ref, dst_ref)
                cur = nxt
            else:
                step(cur, jnp.minimum(qi + 1, nq - 1), src_ref, dst_ref, next_query=True)
    else:
        first_scores()

        def trip(ii, carry):
            cur = jnp.where(ii == 0, qi, other(unroll * ii - 1))
            for s in range(unroll):
                nxt = other(jnp.minimum(unroll * ii + s, nk - 2))
                src_ref, dst_ref = (sta_ref, stb_ref) if s % 2 == 0 else (stb_ref, sta_ref)
                step(cur, nxt, src_ref, dst_ref)
                cur = nxt
            return carry

        lax.fori_loop(0, nk // unroll, trip, 0)

    on = acc_ref[0:LANES, :] * (1.0 / acc_ref[LANES:LANES + 1, :])
    oT = on[:, 0:tq] - lam * on[:, tq:2 * tq]
    ms = jnp.mean(oT * oT, axis=0, keepdims=True)
    oT = oT * lax.rsqrt(ms + RMS_EPS) * g_ref[...] * post
    o_ref[...] = oT.T.astype(BF16)


def _da_tables(slopes):
    jj = np.arange(DA_TK)[:, None]
    ii = np.concatenate([np.arange(DA_TQ), np.arange(DA_TQ)])[None, :]
    dist = jnp.asarray(np.abs(ii - jj).astype(np.float32))
    dbias = -(slopes * LOG2E)[:, None, None] * dist[None]
    j = np.arange(DA_TK)
    terms = []
    for part in ((j >> 5) * 32.0, (j & 31) * 1.0):
        rem = (slopes * LOG2E)[:, None] * jnp.asarray(part.astype(np.float32))[None, :]
        for _ in range(3):
            piece = rem.astype(BF16)
            terms.append(piece)
            rem = rem - piece.astype(F32)
    pos = jnp.stack(terms, axis=-1)
    pos = jnp.concatenate([pos, jnp.zeros(pos.shape[:2] + (LANES - DA_POS_LANES,), BF16)], axis=-1)
    return dbias, pos


def _diff_attention(qkv, scal, subln_col, dbias, pos, B, S):
    T = B * S
    tq, tk = DA_TQ, DA_TK
    nq, nk = S // tq, S // tk
    kern = functools.partial(_da_kernel, nk=nk)
    return pl.pallas_call(
        kern,
        grid=(B, DA_HEADS, nq),
        in_specs=[
            pl.BlockSpec(memory_space=pltpu.SMEM),
            pl.BlockSpec((tq, LANES), lambda b, h, i: (b * nq + i, _DQ + h)),
            pl.BlockSpec((tq, LANES), lambda b, h, i: (b * nq + jnp.minimum(i + 1, nq - 1), _DQ + h)),
            pl.BlockSpec((S, LANES), lambda b, h, i: (b, _DK + h)),
            pl.BlockSpec((S, LANES), lambda b, h, i: (b, _DV + h)),
            pl.BlockSpec((LANES, 1), lambda b, h, i: (0, 0)),
            pl.BlockSpec((1, tk, 2 * tq), lambda b, h, i: (h, 0, 0)),
            pl.BlockSpec((1, tk, LANES), lambda b, h, i: (h, 0, 0)),
        ],
        out_specs=pl.BlockSpec((tq, LANES), lambda b, h, i: (b * nq + i, h)),
        out_shape=jax.ShapeDtypeStruct((T, DA_WIDTH), BF16),
        scratch_shapes=[
            pltpu.VMEM((2 * LANES, 2 * tq), BF16),
            pltpu.VMEM((2 * LANES, 2 * tq), BF16),
            pltpu.VMEM((nk, DA_VROWS, tk), BF16),
            pltpu.VMEM((1, 2 * tq), F32),
            pltpu.VMEM((DA_VROWS, 2 * tq), F32),
            pltpu.VMEM((tk, 2 * tq), F32),
            pltpu.VMEM((tk, 2 * tq), F32),
        ],
        compiler_params=_cparams(("parallel", "parallel", "arbitrary")),
        name="diff_attn",
    )(scal, qkv, qkv, qkv, qkv, subln_col, dbias, pos)


NA_QROWS = 8
NA_KROWS = 16
NA_TQ = NA_QROWS * GRID_W
NA_TK = NA_KROWS * GRID_W


def _na_bias_tables(rpb, rows):
    cols = np.arange(GRID_W)
    cstart = np.clip(cols - NA_WIN_C // 2, 0, GRID_W - NA_WIN_C)
    kc = cols[None, :]
    col_ok = (kc >= cstart[:, None]) & (kc < cstart[:, None] + NA_WIN_C)
    dc = np.clip(kc - cols[:, None] + (NA_WIN_C - 1), 0, 2 * NA_WIN_C - 2)
    t = jnp.where(jnp.asarray(col_ok)[None, None], rpb[:, :, dc] * LOG2E, NEG)
    t = jnp.concatenate([t, jnp.full((NA_HEADS, 1, GRID_W, GRID_W), NEG, F32)], axis=1)
    tables = []
    for r0, w0 in ((0, 0), (rows // 2 // NA_QROWS * NA_QROWS, None), (rows - NA_QROWS, rows - NA_KROWS)):
        if w0 is None:
            r0 = max(NA_QROWS, min(r0, rows - 2 * NA_QROWS))
            w0 = r0 - NA_WIN_R // 2
        r = r0 + np.arange(NA_QROWS)[:, None]
        kr = w0 + np.arange(NA_KROWS)[None, :]
        rs = np.clip(r - NA_WIN_R // 2, 0, rows - NA_WIN_R)
        ok = (kr >= rs) & (kr < rs + NA_WIN_R)
        dr = np.where(ok, kr - r + (NA_WIN_R - 1), 2 * NA_WIN_R - 1)
        tb = t[:, dr]
        tables.append(jnp.transpose(tb, (0, 1, 3, 2, 4)).reshape(NA_HEADS, NA_TQ, NA_TK))
    return jnp.stack(tables)


def _na_kernel(q_ref, k_ref, v_ref, b_ref, o_ref, *, rows):
    j = pl.program_id(2)
    w0 = jnp.clip(j * NA_QROWS - NA_WIN_R // 2, 0, rows - NA_KROWS)
    start = pl.multiple_of(w0 * GRID_W, GRID_W)
    kw = k_ref[pl.ds(start, NA_TK), :]
    vw = v_ref[pl.ds(start, NA_TK), :]
    half = NA_TQ // 2
    lane = lax.broadcasted_iota(jnp.int32, (half, LANES), 1)
    chains = [(qh, hh) for qh in range(2) for hh in range(2)]

    def scores(qh, hh):
        q = q_ref[qh * half:(qh + 1) * half, :]
        msk = (lane < HEAD_DIM) if hh == 0 else (lane >= HEAD_DIM)
        qm = jnp.where(msk, q, jnp.zeros_like(q))
        s = lax.dot_general(qm, kw, (((1,), (1,)), ((), ())), preferred_element_type=F32)
        return s + b_ref[0, hh, qh * half:(qh + 1) * half, :]

    def attend(s):
        m = jnp.max(s, axis=-1, keepdims=True)
        p = jnp.exp2(s - m)
        l = jnp.sum(p, axis=-1, keepdims=True)
        return jnp.dot(p.astype(BF16), vw, preferred_element_type=F32) / l

    outs = {}
    pending = scores(*chains[0])
    for c, chain in enumerate(chains):
        s = pending
        if c + 1 < len(chains):
            pending = scores(*chains[c + 1])
        outs[chain] = attend(s)
    for qh in range(2):
        o_ref[qh * half:(qh + 1) * half, :] = jnp.where(
            lane < HEAD_DIM, outs[(qh, 0)], outs[(qh, 1)]).astype(BF16)


def _neighborhood_attention(qkv, tables, B, S):
    T = B * S
    rows = S // GRID_W
    nblk = rows // NA_QROWS
    kern = functools.partial(_na_kernel, rows=rows)

    def btype(j):
        return jnp.where(j == 0, 0, jnp.where(j == nblk - 1, 2, 1))

    return pl.pallas_call(
        kern,
        grid=(NA_HEADS // 2, B, nblk),
        in_specs=[
            pl.BlockSpec((NA_TQ, LANES), lambda p, b, j: (b * nblk + j, _NQ + p)),
            pl.BlockSpec((S, LANES), lambda p, b, j: (b, _NK + p)),
            pl.BlockSpec((S, LANES), lambda p, b, j: (b, _NV + p)),
            pl.BlockSpec((1, 2, NA_TQ, NA_TK), lambda p, b, j: (btype(j), p, 0, 0)),
        ],
        out_specs=pl.BlockSpec((NA_TQ, LANES), lambda p, b, j: (b * nblk + j, p)),
        out_shape=jax.ShapeDtypeStruct((T, NA_WIDTH), BF16),
        compiler_params=_cparams(("parallel", "parallel", "parallel")),
        name="nbr_attn",
    )(qkv, qkv, qkv, tables)


def _mix_kernel(x_ref, oda_ref, ona_ref, wo_ref, g2_ref, wqt_ref, sk_ref, x1_ref, xn_ref, st_ref):
    x1 = (x_ref[...]
          + jnp.dot(oda_ref[...], wo_ref[0:DA_WIDTH, :], preferred_element_type=F32)
          + jnp.dot(ona_ref[...], wo_ref[DA_WIDTH:, :], preferred_element_type=F32))
    x1_ref[...] = x1
    ms = jnp.mean(x1 * x1, axis=-1, keepdims=True)
    xn = (x1 * lax.rsqrt(ms + RMS_EPS) * g2_ref[...]).astype(BF16)
    xn_ref[...] = xn
    qt = lax.dot_general(wqt_ref[...], xn, (((1,), (1,)), ((), ())), preferred_element_type=F32).astype(BF16)
    for hp in range(2 * PEER_HEADS):
        s = jnp.dot(sk_ref[hp], qt[hp * LANES:(hp + 1) * LANES, :], preferred_element_type=F32)
        for c in range(st_ref.shape[0]):
            st_ref[c, hp * PEER_KEYS:(hp + 1) * PEER_KEYS, :] = s[:, c * ROUTE_L:(c + 1) * ROUTE_L]


def _mix(x2d, oda, ona, wo_bf, g2, wqt_bf, sk_bf, tm=512):
    T = x2d.shape[0]
    nsc = 2 * PEER_HEADS * PEER_KEYS
    return pl.pallas_call(
        _mix_kernel,
        grid=(T // tm,),
        in_specs=[
            pl.BlockSpec((tm, D_MODEL), lambda i: (i, 0)),
            pl.BlockSpec((tm, DA_WIDTH), lambda i: (i, 0)),
            pl.BlockSpec((tm, NA_WIDTH), lambda i: (i, 0)),
            pl.BlockSpec((D_MODEL, D_MODEL), lambda i: (0, 0)),
            pl.BlockSpec((1, D_MODEL), lambda i: (0, 0)),
            pl.BlockSpec((nsc, D_MODEL), lambda i: (0, 0)),
            pl.BlockSpec((2 * PEER_HEADS, PEER_KEYS, LANES), lambda i: (0, 0, 0)),
        ],
        out_specs=[
            pl.BlockSpec((tm, D_MODEL), lambda i: (i, 0)),
            pl.BlockSpec((tm, D_MODEL), lambda i: (i, 0)),
            pl.BlockSpec((tm // ROUTE_L, nsc, ROUTE_L), lambda i: (i, 0, 0)),
        ],
        out_shape=[
            jax.ShapeDtypeStruct((T, D_MODEL), F32),
            jax.ShapeDtypeStruct((T, D_MODEL), BF16),
            jax.ShapeDtypeStruct((T // ROUTE_L, nsc, ROUTE_L), F32),
        ],
        compiler_params=_cparams(("parallel",)),
        name="mix",
    )(x2d, oda, ona, wo_bf, g2, wqt_bf, sk_bf)


def _sort16_network():
    n, pairs, p = 16, [], 1
    while p < n:
        k = p
        while k >= 1:
            j = k % p
            while j <= n - 1 - k:
                for i in range(min(k - 1, n - j - k - 1) + 1):
                    if (i + j) // (2 * p) == (i + j + k) // (2 * p):
                        pairs.append((i + j, i + j + k))
                j += 2 * k
            k //= 2
        p *= 2
    return pairs


def _top16_tiles(val, idx):
    nt = len(val)
    L = val[0].shape[1]
    val, idx = list(val), list(idx)
    for a, b in [(a, b) for a, b in _sort16_network() if b < nt]:
        swap = (val[b] > val[a]) | ((val[b] == val[a]) & (idx[b] < idx[a]))
        val[a], val[b] = jnp.maximum(val[a], val[b]), jnp.minimum(val[a], val[b])
        idx[a], idx[b] = jnp.where(swap, idx[b], idx[a]), jnp.where(swap, idx[a], idx[b])
    slot = lax.broadcasted_iota(jnp.int32, (PEER_TOPK, L), 0)
    vals = jnp.zeros((PEER_TOPK, L), F32)
    idxs = jnp.zeros((PEER_TOPK, L), F32)
    for it in range(PEER_TOPK):
        m = jnp.max(val[0], axis=0, keepdims=True)
        ix = jnp.min(jnp.where(val[0] == m, idx[0], 1e9), axis=0, keepdims=True)
        vals = jnp.where(slot == it, m, vals)
        idxs = jnp.where(slot == it, ix, idxs)
        win = idx[0] == ix
        live = PEER_TOPK - 1 - it
        for k in range(min(nt - 1, live)):
            val[k] = jnp.where(win, val[k + 1], val[k])
            idx[k] = jnp.where(win, idx[k + 1], idx[k])
        if nt - 1 < live:
            val[nt - 1] = jnp.where(win, -jnp.inf, val[nt - 1])
    return vals, idxs


def _top16_rows(s):
    L = s.shape[1]
    sub = lax.broadcasted_iota(jnp.int32, (8, L), 0).astype(F32)
    ntile = s.shape[0] // 8
    return _top16_tiles([s[8 * v:8 * v + 8, :] for v in range(ntile)],
                        [sub + float(8 * v) for v in range(ntile)])


def _route_topk(st_ref, per_head):
    L = st_ref.shape[1]
    ncand = 72
    r = lax.broadcasted_iota(jnp.int32, (ncand, L), 0)
    dup = ((r >= 48) & (r < 53)) | ((r >= 64) & (r < 69))
    pos = jnp.where(r < 16, r,
                    jnp.where(r < 48, (1 + ((r - 16) >> 3)) * PEER_TOPK + ((r - 16) & 7),
                              jnp.where(r < 64, (r - 48) * PEER_TOPK, (r - 64) * PEER_TOPK + 1)))
    pos = jnp.where(dup, 1000 + r, pos).astype(F32)
    pos_tiles = [pos[8 * v:8 * v + 8, :] for v in range(ncand // 8)]
    slot = lax.broadcasted_iota(jnp.int32, (PEER_TOPK, L), 0)
    slotf = slot.astype(F32)
    his, los, gates = [], [], []
    for h in range(PEER_HEADS):
        s1, i1 = _top16_rows(st_ref[(2 * h) * PEER_KEYS:(2 * h + 1) * PEER_KEYS, :])
        s2, i2 = _top16_rows(st_ref[(2 * h + 1) * PEER_KEYS:(2 * h + 2) * PEER_KEYS, :])
        cand = [s1[0:1, :] + s2[0:8, :], s1[0:1, :] + s2[8:16, :]]
        for a in range(1, 5):
            cand.append(s1[a:a + 1, :] + s2[0:8, :])
        cand += [s1[0:8, :] + s2[0:1, :], s1[8:16, :] + s2[0:1, :], s1[0:8, :] + s2[1:2, :]]
        cand = [jnp.where(dup[8 * v:8 * v + 8, :], -jnp.inf, c) if v in (6, 8) else c
                for v, c in enumerate(cand)]
        top, tpos = _top16_tiles(cand, pos_tiles)
        ta = jnp.floor(tpos * (1.0 / PEER_TOPK))
        tb = tpos - ta * PEER_TOPK
        thi = jnp.zeros((PEER_TOPK, L), F32)
        tlo = jnp.zeros((PEER_TOPK, L), F32)
        for a in range(PEER_TOPK):
            thi = jnp.where(ta == float(a), i1[a:a + 1, :], thi)
            tlo = jnp.where(tb == float(a), i2[a:a + 1, :], tlo)
        e = jnp.exp(top - top[0:1, :])
        his.append(thi)
        los.append(tlo)
        gates.append(e / jnp.sum(e, axis=0, keepdims=True))
        per_head(h)
    return (jnp.concatenate(his, axis=0).T, jnp.concatenate(los, axis=0).T,
            jnp.concatenate(gates, axis=0).T)


GATE_ROWS = 8
ROUTE_L = 128


def _route_gates(hi_ref, lo_ref, g_ref, o_ref, t_lo, t_hi):
    nsel = hi_ref.shape[1]
    rid_b = lax.broadcasted_iota(jnp.int32, (PEER_KEYS, nsel), 0).astype(F32).astype(BF16)
    one_b = jnp.ones((PEER_KEYS, nsel), BF16)
    zero_b = jnp.zeros((PEER_KEYS, nsel), BF16)
    for t in range(t_lo, t_hi):
        hi_b = jnp.broadcast_to(hi_ref[t:t + 1, :].astype(BF16), (PEER_KEYS, nsel))
        lo_b = jnp.broadcast_to(lo_ref[t:t + 1, :].astype(BF16), (PEER_KEYS, nsel))
        g_b = jnp.broadcast_to(g_ref[t:t + 1, :].astype(BF16), (PEER_KEYS, nsel))
        qt = jnp.where(rid_b == hi_b, one_b, zero_b)
        pt = jnp.where(rid_b == lo_b, g_b, zero_b)
        gt = lax.dot_general(qt, pt, (((1,), (1,)), ((), ())), preferred_element_type=F32)
        for jb in range(PEER_EXPERTS // PEER_EBLK):
            o_ref[jb, t] = gt[jb * GATE_ROWS:(jb + 1) * GATE_ROWS, :]


def _route_kernel(st_ref, o_ref, hi_s, lo_s, g_s):
    @pl.when(pl.program_id(0) == 0)
    def _no_previous_block():
        hi_s[...] = jnp.zeros_like(hi_s)
        lo_s[...] = jnp.zeros_like(lo_s)
        g_s[...] = jnp.zeros_like(g_s)

    per = hi_s.shape[0] // PEER_HEADS

    def gates_slice(h):
        _route_gates(hi_s, lo_s, g_s, o_ref, h * per, (h + 1) * per)

    hi, lo, g = _route_topk(st_ref.at[0], gates_slice)
    hi_s[...] = hi
    lo_s[...] = lo
    g_s[...] = g


def _route(st):
    nblk, nsc, L = st.shape
    T = nblk * L
    nsel = PEER_HEADS * PEER_TOPK
    njb = PEER_EXPERTS // PEER_EBLK
    return pl.pallas_call(
        _route_kernel,
        grid=(nblk + 1,),
        in_specs=[pl.BlockSpec((1, nsc, L), lambda i: (jnp.minimum(i, nblk - 1), 0, 0))],
        out_specs=pl.BlockSpec((njb, L, GATE_ROWS, PEER_KEYS), lambda i: (0, jnp.maximum(i - 1, 0), 0, 0)),
        out_shape=jax.ShapeDtypeStruct((njb, T, GATE_ROWS, PEER_KEYS), F32),
        scratch_shapes=[pltpu.VMEM((L, nsel), F32)] * 3,
        compiler_params=_cparams(("arbitrary",)),
        name="peer_route",
    )(st)


PEER_TM = 1024
PEER_EBLK = 1024
assert GATE_ROWS * PEER_KEYS == PEER_EBLK


def _gelu_tanh(x):
    return 0.5 * x * (1.0 + jnp.tanh(math.sqrt(2.0 / math.pi) * (x + 0.044715 * (x * x * x))))


def _peer_kernel(x1_ref, xn_ref, ut_ref, v_ref, g3_ref, y_ref):
    j = pl.program_id(1)

    @pl.when(j == 0)
    def _residual():
        y_ref[...] = x1_ref[...]

    hmat = jnp.dot(xn_ref[...], ut_ref[...], preferred_element_type=F32)
    tm = xn_ref.shape[0]
    gates = jnp.concatenate([g3_ref[pl.ds(c, tm, stride=GATE_ROWS), :] for c in range(GATE_ROWS)], axis=1)
    w = (gates * _gelu_tanh(hmat)).astype(BF16)
    y_ref[...] += jnp.dot(w, v_ref[...], preferred_element_type=F32)


def _peer(x1, xn, ut_bf, v_bf, g3):
    T = x1.shape[0]
    tm = PEER_TM
    return pl.pallas_call(
        _peer_kernel,
        grid=(T // tm, PEER_EXPERTS // PEER_EBLK),
        in_specs=[
            pl.BlockSpec((tm, D_MODEL), lambda i, j: (i, 0)),
            pl.BlockSpec((tm, D_MODEL), lambda i, j: (i, 0)),
            pl.BlockSpec((D_MODEL, PEER_EBLK), lambda i, j: (0, j)),
            pl.BlockSpec((PEER_EBLK, D_MODEL), lambda i, j: (j, 0)),
            pl.BlockSpec((tm * GATE_ROWS, PEER_KEYS), lambda i, j: (j * (T // tm) + i, 0)),
        ],
        out_specs=pl.BlockSpec((tm, D_MODEL), lambda i, j: (i, 0)),
        out_shape=jax.ShapeDtypeStruct((T, D_MODEL), F32),
        compiler_params=_cparams(("parallel", "arbitrary")),
        name="peer_mlp",
    )(x1, xn, ut_bf, v_bf, g3.reshape(-1, PEER_KEYS))


def _prepare_params(layer_idx, ln1_g, w_in, da_q_norm_g, da_k_norm_g, lam_q1, lam_k1, lam_q2, lam_k2,
                    da_subln_g, na_q_norm_g, na_k_norm_g, na_rpb, w_out, ln2_g,
                    peer_w_query, peer_sub_keys, peer_u, peer_v):
    scale = HEAD_DIM ** -0.5
    gn = jnp.stack([jnp.tile(da_q_norm_g, 8) * (scale * LOG2E), jnp.tile(da_k_norm_g, 8),
                    jnp.tile(na_q_norm_g, 8) * (scale * LOG2E), jnp.tile(na_k_norm_g, 8)]).astype(F32)
    grp = np.arange(512) // HEAD_DIM
    gm = jnp.asarray((grp[:, None] == grp[None, :]).astype(np.float32) / HEAD_DIM, BF16)
    lambda_init = 0.8 - 0.6 * math.exp(-0.3 * layer_idx)
    lam = (jnp.exp(jnp.sum(lam_q1.astype(F32) * lam_k1.astype(F32)))
           - jnp.exp(jnp.sum(lam_q2.astype(F32) * lam_k2.astype(F32))) + lambda_init)
    slopes = jnp.asarray([2.0 ** (-8.0 * (h + 1) / DA_HEADS) for h in range(DA_HEADS)], F32)
    scal = jnp.concatenate([slopes, lam.reshape(1), jnp.full((1,), 1.0 - lambda_init, F32)])
    dbias, pos = _da_tables(slopes)
    return dict(
        g1=ln1_g.reshape(1, D_MODEL), w_in=w_in.astype(BF16), gn=gn, gm=gm, scal=scal,
        subln=da_subln_g.reshape(LANES, 1), dbias=dbias, pos=pos, rpb=na_rpb,
        w_out=w_out.astype(BF16), g2=ln2_g.reshape(1, D_MODEL), wqt=peer_w_query.T.astype(BF16),
        sk=peer_sub_keys.reshape(2 * PEER_HEADS, PEER_KEYS, LANES).astype(BF16),
        ut=peer_u.T.astype(BF16), v=peer_v.astype(BF16))


def _encoder_layer(x, p, tables):
    B, S, _ = x.shape
    x2d = x.reshape(B * S, D_MODEL)
    qkv = _inproj(x2d, p["g1"], p["w_in"], p["gn"], p["gm"])
    oda = _diff_attention(qkv, p["scal"], p["subln"], p["dbias"], p["pos"], B, S)
    ona = _neighborhood_attention(qkv, tables, B, S)
    x1, xn, st = _mix(x2d, oda, ona, p["w_out"], p["g2"], p["wqt"], p["sk"])
    g3 = _route(st)
    y = _peer(x1, xn, p["ut"], p["v"], g3)
    return y.reshape(B, S, D_MODEL)


def kernel(x_prompt, x_sample, ln1_g, w_in, da_q_norm_g, da_k_norm_g, da_lambda_q1, da_lambda_k1,
           da_lambda_q2, da_lambda_k2, da_subln_g, na_q_norm_g, na_k_norm_g, na_rpb, w_out, ln2_g,
           peer_w_query, peer_sub_keys, peer_u, peer_v):
    hp, hs = x_prompt, x_sample
    for l in range(ln1_g.shape[0]):
        p = _prepare_params(l, ln1_g[l], w_in[l], da_q_norm_g[l], da_k_norm_g[l], da_lambda_q1[l],
                            da_lambda_k1[l], da_lambda_q2[l], da_lambda_k2[l], da_subln_g[l],
                            na_q_norm_g[l], na_k_norm_g[l], na_rpb[l], w_out[l], ln2_g[l],
                            peer_w_query[l], peer_sub_keys[l], peer_u[l], peer_v[l])
        outs = []
        for x in (hp, hs):
            tables = _na_bias_tables(p["rpb"], x.shape[1] // GRID_W)
            outs.append(_encoder_layer(x, p, tables))
        hp, hs = outs
    return (hp, hs)
```

```python
import functools
import math

import numpy as np
import jax
import jax.numpy as jnp
from jax import lax
from jax.experimental import pallas as pl
from jax.experimental.pallas import tpu as pltpu

F32 = jnp.float32
BF16 = jnp.bfloat16

D_MODEL = 1024
HEAD_DIM = 64
LANES = 128
MXU_N = 256
DA_HEADS = 4
DA_WIDTH = 512
NA_HEADS = 8
NA_WIDTH = 512
GRID_W = 64
NA_WIN_R = 8
NA_WIN_C = 16
IN_COLS = 3 * DA_WIDTH + 3 * NA_WIDTH
PEER_HEADS = 8
PEER_KEYS = 128
PEER_QDIM = 256
PEER_TOPK = 16
PEER_EXPERTS = PEER_KEYS * PEER_KEYS
RMS_EPS = 1e-6
NEG = -1e30
VMEM_LIMIT = 56 * 1024 * 1024

_DQ, _DK, _DV = 0, 4, 8
_NQ, _NK, _NV = 12, 16, 20


def _cparams(sem):
    return pltpu.CompilerParams(dimension_semantics=sem, vmem_limit_bytes=VMEM_LIMIT)


def _group_ms(x, gm):
    sq = x * x
    hi = sq.astype(BF16)
    lo = (sq - hi.astype(F32)).astype(BF16)
    return (jnp.dot(hi, gm, preferred_element_type=F32)
            + jnp.dot(lo, gm, preferred_element_type=F32))


def _inproj_kernel(x_ref, g1_ref, w_ref, gn_ref, gm_ref, o_ref):
    x = x_ref[...]
    ms = jnp.mean(x * x, axis=-1, keepdims=True)
    xn = (x * lax.rsqrt(ms + RMS_EPS) * g1_ref[...]).astype(BF16)
    proj = jnp.dot(xn, w_ref[...], preferred_element_type=F32)
    gm = gm_ref[...]
    for sec, row in ((0, 0), (1, 1), (3, 2), (4, 3)):
        xs = proj[:, sec * 512:(sec + 1) * 512]
        y = xs * lax.rsqrt(_group_ms(xs, gm) + RMS_EPS) * gn_ref[row:row + 1, :]
        o_ref[:, sec * 512:(sec + 1) * 512] = y.astype(BF16)
    for sec in (2, 5):
        o_ref[:, sec * 512:(sec + 1) * 512] = proj[:, sec * 512:(sec + 1) * 512].astype(BF16)


def _inproj(x2d, g1, w_in_bf, gn, gm, tm=512):
    T = x2d.shape[0]
    return pl.pallas_call(
        _inproj_kernel,
        grid=(T // tm,),
        in_specs=[
            pl.BlockSpec((tm, D_MODEL), lambda i: (i, 0)),
            pl.BlockSpec((1, D_MODEL), lambda i: (0, 0)),
            pl.BlockSpec((D_MODEL, IN_COLS), lambda i: (0, 0)),
            pl.BlockSpec((4, 512), lambda i: (0, 0)),
            pl.BlockSpec((512, 512), lambda i: (0, 0)),
        ],
        out_specs=pl.BlockSpec((tm, IN_COLS), lambda i: (i, 0)),
        out_shape=jax.ShapeDtypeStruct((T, IN_COLS), BF16),
        compiler_params=_cparams(("parallel",)),
        name="inproj",
    )(x2d, g1, w_in_bf, gn, gm)


DA_TQ = 512
DA_TK = 512
DA_CHUNK = MXU_N
DA_POS_LANES = 6
DA_VROWS = 144
LOG2E = math.log2(math.e)


def _da_kernel(sc_ref, q_ref, qn_ref, k_ref, v_ref, g_ref, db_ref, pos_ref, o_ref,
               qst_ref, qstn_ref, vt_ref, m_ref, acc_ref, sta_ref, stb_ref, *, nk):
    tq, tk = DA_TQ, DA_TK
    h = pl.program_id(1)
    qi = pl.program_id(2)
    nq = pl.num_programs(2)
    slope = sc_ref[h]
    lam = sc_ref[4]
    post = sc_ref[5]
    unroll = nk if nk <= 16 else (8 if nk % 8 == 0 else 2)
    handoff = unroll == nk

    @pl.when(qi == 0)
    def _transpose_values():
        ones_row = jnp.where(lax.broadcasted_iota(jnp.int32, (DA_VROWS - LANES, tk), 0) == 0, 1.0, 0.0)
        for c in range(nk):
            vt_ref[c, 0:LANES, :] = v_ref[c * tk:(c + 1) * tk, :].astype(F32).T.astype(BF16)
            vt_ref[c, LANES:DA_VROWS, :] = ones_row.astype(BF16)

    def query_operand(dst_ref, src_ref):
        qT = src_ref[...].astype(F32).T
        row = lax.broadcasted_iota(jnp.int32, (LANES, tq), 0)
        dst_ref[0:LANES, 0:tq] = jnp.where(row < HEAD_DIM, qT, 0.0).astype(BF16)
        dst_ref[0:LANES, tq:2 * tq] = jnp.where(row >= HEAD_DIM, qT, 0.0).astype(BF16)
        row2 = lax.broadcasted_iota(jnp.int32, (LANES, 2 * tq), 0)
        dst_ref[LANES:2 * LANES, :] = jnp.where(row2 < DA_POS_LANES, 1.0, 0.0).astype(BF16)

    query_operand(qst_ref, q_ref)
    if handoff:
        query_operand(qstn_ref, qn_ref)
    m_ref[...] = jnp.full((1, 2 * tq), NEG, F32)
    acc_ref[...] = jnp.zeros((DA_VROWS, 2 * tq), F32)

    pos = pos_ref[0].astype(F32)
    col = lax.broadcasted_iota(jnp.int32, (1, 2 * tq), 1)
    icol = jnp.where(col >= tq, col - tq, col).astype(F32)

    nch = 2 * tq // DA_CHUNK

    def cols(n):
        return slice(n * DA_CHUNK, (n + 1) * DA_CHUNK)

    def side(blk):
        return jnp.where(qi > blk, 1.0, jnp.where(qi < blk, -1.0, 0.0))

    def key_operand(blk, sgn):
        k0 = pl.multiple_of(blk * tk, tk)
        return jnp.concatenate([k_ref[pl.ds(k0, tk), :], (pos * sgn).astype(BF16)], axis=1)

    def softmax_pv(n, src_ref, vtb, cq):
        sl = cols(n)
        st = src_ref[:, sl]
        c_n = cq[:, sl]
        m_old = m_ref[:, sl]
        m_new = jnp.maximum(m_old, jnp.max(st, axis=0, keepdims=True) + c_n)
        p = jnp.exp2(st - (m_new - c_n)).astype(BF16)
        alpha = jnp.exp2(m_old - m_new)
        acc_ref[:, sl] = alpha * acc_ref[:, sl] + jnp.dot(vtb, p, preferred_element_type=F32)
        m_ref[:, sl] = m_new

    def diagonal_scores(n, ka, qop_ref):
        return jnp.dot(ka, qop_ref[:, cols(n)], preferred_element_type=F32) + db_ref[0, :, cols(n)]

    def step(cur, nxt, src_ref, dst_ref, next_query=False):
        ka = key_operand(nxt, 0.0 if next_query else side(nxt))
        vtb = vt_ref[cur]
        cq = (-side(cur) * slope * LOG2E) * ((qi * tq - cur * tk).astype(F32) + icol)

        def scores(n):
            if next_query:
                dst_ref[:, cols(n)] = diagonal_scores(n, ka, qstn_ref)
            else:
                dst_ref[:, cols(n)] = jnp.dot(ka, qst_ref[:, cols(n)], preferred_element_type=F32)

        scores(0)
        scores(1)
        for n in range(nch):
            softmax_pv(n, src_ref, vtb, cq)
            if n + 2 < nch:
                scores(n + 2)

    def other(x):
        return x + (x >= qi).astype(jnp.int32)

    def first_scores():
        ka0 = key_operand(qi, 0.0)
        for n in range(nch):
            sta_ref[:, cols(n)] = diagonal_scores(n, ka0, qst_ref)

    if handoff:
        pl.when(qi == 0)(first_scores)
        cur = qi
        for s in range(nk):
            src_ref, dst_ref = (sta_ref, stb_ref) if s % 2 == 0 else (stb_ref, sta_ref)
            if s < nk - 1:
                nxt = other(s)
                step(cur, nxt, src_ref, dst_ref)
                cur = nxt
            else:
                step(cur, jnp.minimum(qi + 1, nq - 1), src_ref, dst_ref, next_query=True)
    else:
        first_scores()

        def trip(ii, carry):
            cur = jnp.where(ii == 0, qi, other(unroll * ii - 1))
            for s in range(unroll):
                nxt = other(jnp.minimum(unroll * ii + s, nk - 2))
                src_ref, dst_ref = (sta_ref, stb_ref) if s % 2 == 0 else (stb_ref, sta_ref)
                step(cur, nxt, src_ref, dst_ref)
                cur = nxt
            return carry

        lax.fori_loop(0, nk // unroll, trip, 0)

    on = acc_ref[0:LANES, :] * (1.0 / acc_ref[LANES:LANES + 1, :])
    oT = on[:, 0:tq] - lam * on[:, tq:2 * tq]
    ms = jnp.mean(oT * oT, axis=0, keepdims=True)
    oT = oT * lax.rsqrt(ms + RMS_EPS) * g_ref[...] * post
    o_ref[...] = oT.T.astype(BF16)


def _da_tables(slopes):
    jj = np.arange(DA_TK)[:, None]
    ii = np.concatenate([np.arange(DA_TQ), np.arange(DA_TQ)])[None, :]
    dist = jnp.asarray(np.abs(ii - jj).astype(np.float32))
    dbias = -(slopes * LOG2E)[:, None, None] * dist[None]
    j = np.arange(DA_TK)
    terms = []
    for part in ((j >> 5) * 32.0, (j & 31) * 1.0):
        rem = (slopes * LOG2E)[:, None] * jnp.asarray(part.astype(np.float32))[None, :]
        for _ in range(3):
            piece = rem.astype(BF16)
            terms.append(piece)
            rem = rem - piece.astype(F32)
    pos = jnp.stack(terms, axis=-1)
    pos = jnp.concatenate([pos, jnp.zeros(pos.shape[:2] + (LANES - DA_POS_LANES,), BF16)], axis=-1)
    return dbias, pos


def _diff_attention(qkv, scal, subln_col, dbias, pos, B, S):
    T = B * S
    tq, tk = DA_TQ, DA_TK
    nq, nk = S // tq, S // tk
    kern = functools.partial(_da_kernel, nk=nk)
    return pl.pallas_call(
        kern,
        grid=(B, DA_HEADS, nq),
        in_specs=[
            pl.BlockSpec(memory_space=pltpu.SMEM),
            pl.BlockSpec((tq, LANES), lambda b, h, i: (b * nq + i, _DQ + h)),
            pl.BlockSpec((tq, LANES), lambda b, h, i: (b * nq + jnp.minimum(i + 1, nq - 1), _DQ + h)),
            pl.BlockSpec((S, LANES), lambda b, h, i: (b, _DK + h)),
            pl.BlockSpec((S, LANES), lambda b, h, i: (b, _DV + h)),
            pl.BlockSpec((LANES, 1), lambda b, h, i: (0, 0)),
            pl.BlockSpec((1, tk, 2 * tq), lambda b, h, i: (h, 0, 0)),
            pl.BlockSpec((1, tk, LANES), lambda b, h, i: (h, 0, 0)),
        ],
        out_specs=pl.BlockSpec((tq, LANES), lambda b, h, i: (b * nq + i, h)),
        out_shape=jax.ShapeDtypeStruct((T, DA_WIDTH), BF16),
        scratch_shapes=[
            pltpu.VMEM((2 * LANES, 2 * tq), BF16),
            pltpu.VMEM((2 * LANES, 2 * tq), BF16),
            pltpu.VMEM((nk, DA_VROWS, tk), BF16),
            pltpu.VMEM((1, 2 * tq), F32),
            pltpu.VMEM((DA_VROWS, 2 * tq), F32),
            pltpu.VMEM((tk, 2 * tq), F32),
            pltpu.VMEM((tk, 2 * tq), F32),
        ],
        compiler_params=_cparams(("parallel", "parallel", "arbitrary")),
        name="diff_attn",
    )(scal, qkv, qkv, qkv, qkv, subln_col, dbias, pos)


NA_QROWS = 8
NA_KROWS = 16
NA_TQ = NA_QROWS * GRID_W
NA_TK = NA_KROWS * GRID_W


def _na_bias_tables(rpb, rows):
    cols = np.arange(GRID_W)
    cstart = np.clip(cols - NA_WIN_C // 2, 0, GRID_W - NA_WIN_C)
    kc = cols[None, :]
    col_ok = (kc >= cstart[:, None]) & (kc < cstart[:, None] + NA_WIN_C)
    dc = np.clip(kc - cols[:, None] + (NA_WIN_C - 1), 0, 2 * NA_WIN_C - 2)
    t = jnp.where(jnp.asarray(col_ok)[None, None], rpb[:, :, dc] * LOG2E, NEG)
    t = jnp.concatenate([t, jnp.full((NA_HEADS, 1, GRID_W, GRID_W), NEG, F32)], axis=1)
    tables = []
    for r0, w0 in ((0, 0), (rows // 2 // NA_QROWS * NA_QROWS, None), (rows - NA_QROWS, rows - NA_KROWS)):
        if w0 is None:
            r0 = max(NA_QROWS, min(r0, rows - 2 * NA_QROWS))
            w0 = r0 - NA_WIN_R // 2
        r = r0 + np.arange(NA_QROWS)[:, None]
        kr = w0 + np.arange(NA_KROWS)[None, :]
        rs = np.clip(r - NA_WIN_R // 2, 0, rows - NA_WIN_R)
        ok = (kr >= rs) & (kr < rs + NA_WIN_R)
        dr = np.where(ok, kr - r + (NA_WIN_R - 1), 2 * NA_WIN_R - 1)
        tb = t[:, dr]
        tables.append(jnp.transpose(tb, (0, 1, 3, 2, 4)).reshape(NA_HEADS, NA_TQ, NA_TK))
    return jnp.stack(tables)


def _na_kernel(q_ref, k_ref, v_ref, b_ref, o_ref, *, rows):
    j = pl.program_id(2)
    w0 = jnp.clip(j * NA_QROWS - NA_WIN_R // 2, 0, rows - NA_KROWS)
    start = pl.multiple_of(w0 * GRID_W, GRID_W)
    kw = k_ref[pl.ds(start, NA_TK), :]
    vw = v_ref[pl.ds(start, NA_TK), :]
    half = NA_TQ // 2
    lane = lax.broadcasted_iota(jnp.int32, (half, LANES), 1)
    chains = [(qh, hh) for qh in range(2) for hh in range(2)]

    def scores(qh, hh):
        q = q_ref[qh * half:(qh + 1) * half, :]
        msk = (lane < HEAD_DIM) if hh == 0 else (lane >= HEAD_DIM)
        qm = jnp.where(msk, q, jnp.zeros_like(q))
        s = lax.dot_general(qm, kw, (((1,), (1,)), ((), ())), preferred_element_type=F32)
        return s + b_ref[0, hh, qh * half:(qh + 1) * half, :]

    def attend(s):
        m = jnp.max(s, axis=-1, keepdims=True)
        p = jnp.exp2(s - m)
        l = jnp.sum(p, axis=-1, keepdims=True)
        return jnp.dot(p.astype(BF16), vw, preferred_element_type=F32) / l

    outs = {}
    pending = scores(*chains[0])
    for c, chain in enumerate(chains):
        s = pending
        if c + 1 < len(chains):
            pending = scores(*chains[c + 1])
        outs[chain] = attend(s)
    for qh in range(2):
        o_ref[qh * half:(qh + 1) * half, :] = jnp.where(
            lane < HEAD_DIM, outs[(qh, 0)], outs[(qh, 1)]).astype(BF16)


def _neighborhood_attention(qkv, tables, B, S):
    T = B * S
    rows = S // GRID_W
    nblk = rows // NA_QROWS
    kern = functools.partial(_na_kernel, rows=rows)

    def btype(j):
        return jnp.where(j == 0, 0, jnp.where(j == nblk - 1, 2, 1))

    return pl.pallas_call(
        kern,
        grid=(NA_HEADS // 2, B, nblk),
        in_specs=[
            pl.BlockSpec((NA_TQ, LANES), lambda p, b, j: (b * nblk + j, _NQ + p)),
            pl.BlockSpec((S, LANES), lambda p, b, j: (b, _NK + p)),
            pl.BlockSpec((S, LANES), lambda p, b, j: (b, _NV + p)),
            pl.BlockSpec((1, 2, NA_TQ, NA_TK), lambda p, b, j: (btype(j), p, 0, 0)),
        ],
        out_specs=pl.BlockSpec((NA_TQ, LANES), lambda p, b, j: (b * nblk + j, p)),
        out_shape=jax.ShapeDtypeStruct((T, NA_WIDTH), BF16),
        compiler_params=_cparams(("parallel", "parallel", "parallel")),
        name="nbr_attn",
    )(qkv, qkv, qkv, tables)


def _mix_kernel(x_ref, oda_ref, ona_ref, wo_ref, g2_ref, wqt_ref, sk_ref, x1_ref, xn_ref, st_ref):
    x1 = (x_ref[...]
          + jnp.dot(oda_ref[...], wo_ref[0:DA_WIDTH, :], preferred_element_type=F32)
          + jnp.dot(ona_ref[...], wo_ref[DA_WIDTH:, :], preferred_element_type=F32))
    x1_ref[...] = x1
    ms = jnp.mean(x1 * x1, axis=-1, keepdims=True)
    xn = (x1 * lax.rsqrt(ms + RMS_EPS) * g2_ref[...]).astype(BF16)
    xn_ref[...] = xn
    qt = lax.dot_general(wqt_ref[...], xn, (((1,), (1,)), ((), ())), preferred_element_type=F32).astype(BF16)
    for hp in range(2 * PEER_HEADS):
        st_ref[hp * PEER_KEYS:(hp + 1) * PEER_KEYS, :] = jnp.dot(
            sk_ref[hp], qt[hp * LANES:(hp + 1) * LANES, :], preferred_element_type=F32)


def _mix(x2d, oda, ona, wo_bf, g2, wqt_bf, sk_bf, tm=512):
    T = x2d.shape[0]
    nsc = 2 * PEER_HEADS * PEER_KEYS
    return pl.pallas_call(
        _mix_kernel,
        grid=(T // tm,),
        in_specs=[
            pl.BlockSpec((tm, D_MODEL), lambda i: (i, 0)),
            pl.BlockSpec((tm, DA_WIDTH), lambda i: (i, 0)),
            pl.BlockSpec((tm, NA_WIDTH), lambda i: (i, 0)),
            pl.BlockSpec((D_MODEL, D_MODEL), lambda i: (0, 0)),
            pl.BlockSpec((1, D_MODEL), lambda i: (0, 0)),
            pl.BlockSpec((nsc, D_MODEL), lambda i: (0, 0)),
            pl.BlockSpec((2 * PEER_HEADS, PEER_KEYS, LANES), lambda i: (0, 0, 0)),
        ],
        out_specs=[
            pl.BlockSpec((tm, D_MODEL), lambda i: (i, 0)),
            pl.BlockSpec((tm, D_MODEL), lambda i: (i, 0)),
            pl.BlockSpec((nsc, tm), lambda i: (0, i)),
        ],
        out_shape=[
            jax.ShapeDtypeStruct((T, D_MODEL), F32),
            jax.ShapeDtypeStruct((T, D_MODEL), BF16),
            jax.ShapeDtypeStruct((nsc, T), F32),
        ],
        compiler_params=_cparams(("parallel",)),
        name="mix",
    )(x2d, oda, ona, wo_bf, g2, wqt_bf, sk_bf)


def _sort16_network():
    n, pairs, p = 16, [], 1
    while p < n:
        k = p
        while k >= 1:
            j = k % p
            while j <= n - 1 - k:
                for i in range(min(k - 1, n - j - k - 1) + 1):
                    if (i + j) // (2 * p) == (i + j + k) // (2 * p):
                        pairs.append((i + j, i + j + k))
                j += 2 * k
            k //= 2
        p *= 2
    return pairs


def _top16_tiles(val, idx):
    nt = len(val)
    L = val[0].shape[1]
    val, idx = list(val), list(idx)
    for a, b in [(a, b) for a, b in _sort16_network() if b < nt]:
        swap = (val[b] > val[a]) | ((val[b] == val[a]) & (idx[b] < idx[a]))
        val[a], val[b] = jnp.maximum(val[a], val[b]), jnp.minimum(val[a], val[b])
        idx[a], idx[b] = jnp.where(swap, idx[b], idx[a]), jnp.where(swap, idx[a], idx[b])
    slot = lax.broadcasted_iota(jnp.int32, (PEER_TOPK, L), 0)
    vals = jnp.zeros((PEER_TOPK, L), F32)
    idxs = jnp.zeros((PEER_TOPK, L), F32)
    for it in range(PEER_TOPK):
        m = jnp.max(val[0], axis=0, keepdims=True)
        ix = jnp.min(jnp.where(val[0] == m, idx[0], 1e9), axis=0, keepdims=True)
        vals = jnp.where(slot == it, m, vals)
        idxs = jnp.where(slot == it, ix, idxs)
        win = idx[0] == ix
        live = PEER_TOPK - 1 - it
        for k in range(min(nt - 1, live)):
            val[k] = jnp.where(win, val[k + 1], val[k])
            idx[k] = jnp.where(win, idx[k + 1], idx[k])
        if nt - 1 < live:
            val[nt - 1] = jnp.where(win, -jnp.inf, val[nt - 1])
    return vals, idxs


def _top16_rows(s):
    L = s.shape[1]
    sub = lax.broadcasted_iota(jnp.int32, (8, L), 0).astype(F32)
    ntile = s.shape[0] // 8
    return _top16_tiles([s[8 * v:8 * v + 8, :] for v in range(ntile)],
                        [sub + float(8 * v) for v in range(ntile)])


def _route_topk(st_ref, per_head):
    L = st_ref.shape[1]
    ncand = 72
    r = lax.broadcasted_iota(jnp.int32, (ncand, L), 0)
    dup = ((r >= 48) & (r < 53)) | ((r >= 64) & (r < 69))
    pos = jnp.where(r < 16, r,
                    jnp.where(r < 48, (1 + ((r - 16) >> 3)) * PEER_TOPK + ((r - 16) & 7),
                              jnp.where(r < 64, (r - 48) * PEER_TOPK, (r - 64) * PEER_TOPK + 1)))
    pos = jnp.where(dup, 1000 + r, pos).astype(F32)
    pos_tiles = [pos[8 * v:8 * v + 8, :] for v in range(ncand // 8)]
    slot = lax.broadcasted_iota(jnp.int32, (PEER_TOPK, L), 0)
    slotf = slot.astype(F32)
    his, los, gates = [], [], []
    for h in range(PEER_HEADS):
        s1, i1 = _top16_rows(st_ref[(2 * h) * PEER_KEYS:(2 * h + 1) * PEER_KEYS, :])
        s2, i2 = _top16_rows(st_ref[(2 * h + 1) * PEER_KEYS:(2 * h + 2) * PEER_KEYS, :])
        cand = [s1[0:1, :] + s2[0:8, :], s1[0:1, :] + s2[8:16, :]]
        for a in range(1, 5):
            cand.append(s1[a:a + 1, :] + s2[0:8, :])
        cand += [s1[0:8, :] + s2[0:1, :], s1[8:16, :] + s2[0:1, :], s1[0:8, :] + s2[1:2, :]]
        cand = [jnp.where(dup[8 * v:8 * v + 8, :], -jnp.inf, c) if v in (6, 8) else c
                for v, c in enumerate(cand)]
        top, tpos = _top16_tiles(cand, pos_tiles)
        ta = jnp.floor(tpos * (1.0 / PEER_TOPK))
        tb = tpos - ta * PEER_TOPK
        thi = jnp.zeros((PEER_TOPK, L), F32)
        tlo = jnp.zeros((PEER_TOPK, L), F32)
        for a in range(PEER_TOPK):
            thi = jnp.where(ta == float(a), i1[a:a + 1, :], thi)
            tlo = jnp.where(tb == float(a), i2[a:a + 1, :], tlo)
        e = jnp.exp(top - top[0:1, :])
        his.append(thi)
        los.append(tlo)
        gates.append(e / jnp.sum(e, axis=0, keepdims=True))
        per_head(h)
    return (jnp.concatenate(his, axis=0).T, jnp.concatenate(los, axis=0).T,
            jnp.concatenate(gates, axis=0).T)


GATE_ROWS = 8
ROUTE_L = 128


def _route_gates(hi_ref, lo_ref, g_ref, o_ref, t_lo, t_hi):
    nsel = hi_ref.shape[1]
    rid_b = lax.broadcasted_iota(jnp.int32, (PEER_KEYS, nsel), 0).astype(F32).astype(BF16)
    one_b = jnp.ones((PEER_KEYS, nsel), BF16)
    zero_b = jnp.zeros((PEER_KEYS, nsel), BF16)
    for t in range(t_lo, t_hi):
        hi_b = jnp.broadcast_to(hi_ref[t:t + 1, :].astype(BF16), (PEER_KEYS, nsel))
        lo_b = jnp.broadcast_to(lo_ref[t:t + 1, :].astype(BF16), (PEER_KEYS, nsel))
        g_b = jnp.broadcast_to(g_ref[t:t + 1, :].astype(BF16), (PEER_KEYS, nsel))
        qt = jnp.where(rid_b == hi_b, one_b, zero_b)
        pt = jnp.where(rid_b == lo_b, g_b, zero_b)
        gt = lax.dot_general(qt, pt, (((1,), (1,)), ((), ())), preferred_element_type=F32)
        for jb in range(PEER_EXPERTS // PEER_EBLK):
            o_ref[jb, t] = gt[jb * GATE_ROWS:(jb + 1) * GATE_ROWS, :]


def _route_kernel(st_ref, o_ref, hi_s, lo_s, g_s):
    @pl.when(pl.program_id(0) == 0)
    def _no_previous_block():
        hi_s[...] = jnp.zeros_like(hi_s)
        lo_s[...] = jnp.zeros_like(lo_s)
        g_s[...] = jnp.zeros_like(g_s)

    per = hi_s.shape[0] // PEER_HEADS

    def gates_slice(h):
        _route_gates(hi_s, lo_s, g_s, o_ref, h * per, (h + 1) * per)

    hi, lo, g = _route_topk(st_ref, gates_slice)
    hi_s[...] = hi
    lo_s[...] = lo
    g_s[...] = g


def _route(st):
    nsc, T = st.shape
    L = ROUTE_L
    nblk = T // L
    nsel = PEER_HEADS * PEER_TOPK
    njb = PEER_EXPERTS // PEER_EBLK
    return pl.pallas_call(
        _route_kernel,
        grid=(nblk + 1,),
        in_specs=[pl.BlockSpec((nsc, L), lambda i: (0, jnp.minimum(i, nblk - 1)))],
        out_specs=pl.BlockSpec((njb, L, GATE_ROWS, PEER_KEYS), lambda i: (0, jnp.maximum(i - 1, 0), 0, 0)),
        out_shape=jax.ShapeDtypeStruct((njb, T, GATE_ROWS, PEER_KEYS), F32),
        scratch_shapes=[pltpu.VMEM((L, nsel), F32)] * 3,
        compiler_params=_cparams(("arbitrary",)),
        name="peer_route",
    )(st)


PEER_TM = 1024
PEER_EBLK = 1024
assert GATE_ROWS * PEER_KEYS == PEER_EBLK


def _gelu_tanh(x):
    return 0.5 * x * (1.0 + jnp.tanh(math.sqrt(2.0 / math.pi) * (x + 0.044715 * (x * x * x))))


def _peer_kernel(x1_ref, xn_ref, ut_ref, v_ref, g3_ref, y_ref):
    j = pl.program_id(1)

    @pl.when(j == 0)
    def _residual():
        y_ref[...] = x1_ref[...]

    hmat = jnp.dot(xn_ref[...], ut_ref[...], preferred_element_type=F32)
    tm = xn_ref.shape[0]
    gates = jnp.concatenate([g3_ref[pl.ds(c, tm, stride=GATE_ROWS), :] for c in range(GATE_ROWS)], axis=1)
    w = (gates * _gelu_tanh(hmat)).astype(BF16)
    y_ref[...] += jnp.dot(w, v_ref[...], preferred_element_type=F32)


def _peer(x1, xn, ut_bf, v_bf, g3):
    T = x1.shape[0]
    tm = PEER_TM
    return pl.pallas_call(
        _peer_kernel,
        grid=(T // tm, PEER_EXPERTS // PEER_EBLK),
        in_specs=[
            pl.BlockSpec((tm, D_MODEL), lambda i, j: (i, 0)),
            pl.BlockSpec((tm, D_MODEL), lambda i, j: (i, 0)),
            pl.BlockSpec((D_MODEL, PEER_EBLK), lambda i, j: (0, j)),
            pl.BlockSpec((PEER_EBLK, D_MODEL), lambda i, j: (j, 0)),
            pl.BlockSpec((tm * GATE_ROWS, PEER_KEYS), lambda i, j: (j * (T // tm) + i, 0)),
        ],
        out_specs=pl.BlockSpec((tm, D_MODEL), lambda i, j: (i, 0)),
        out_shape=jax.ShapeDtypeStruct((T, D_MODEL), F32),
        compiler_params=_cparams(("parallel", "arbitrary")),
        name="peer_mlp",
    )(x1, xn, ut_bf, v_bf, g3.reshape(-1, PEER_KEYS))


def _prepare_params(layer_idx, ln1_g, w_in, da_q_norm_g, da_k_norm_g, lam_q1, lam_k1, lam_q2, lam_k2,
                    da_subln_g, na_q_norm_g, na_k_norm_g, na_rpb, w_out, ln2_g,
                    peer_w_query, peer_sub_keys, peer_u, peer_v):
    scale = HEAD_DIM ** -0.5
    gn = jnp.stack([jnp.tile(da_q_norm_g, 8) * (scale * LOG2E), jnp.tile(da_k_norm_g, 8),
                    jnp.tile(na_q_norm_g, 8) * (scale * LOG2E), jnp.tile(na_k_norm_g, 8)]).astype(F32)
    grp = np.arange(512) // HEAD_DIM
    gm = jnp.asarray((grp[:, None] == grp[None, :]).astype(np.float32) / HEAD_DIM, BF16)
    lambda_init = 0.8 - 0.6 * math.exp(-0.3 * layer_idx)
    lam = (jnp.exp(jnp.sum(lam_q1.astype(F32) * lam_k1.astype(F32)))
           - jnp.exp(jnp.sum(lam_q2.astype(F32) * lam_k2.astype(F32))) + lambda_init)
    slopes = jnp.asarray([2.0 ** (-8.0 * (h + 1) / DA_HEADS) for h in range(DA_HEADS)], F32)
    scal = jnp.concatenate([slopes, lam.reshape(1), jnp.full((1,), 1.0 - lambda_init, F32)])
    dbias, pos = _da_tables(slopes)
    return dict(
        g1=ln1_g.reshape(1, D_MODEL), w_in=w_in.astype(BF16), gn=gn, gm=gm, scal=scal,
        subln=da_subln_g.reshape(LANES, 1), dbias=dbias, pos=pos, rpb=na_rpb,
        w_out=w_out.astype(BF16), g2=ln2_g.reshape(1, D_MODEL), wqt=peer_w_query.T.astype(BF16),
        sk=peer_sub_keys.reshape(2 * PEER_HEADS, PEER_KEYS, LANES).astype(BF16),
        ut=peer_u.T.astype(BF16), v=peer_v.astype(BF16))


def _encoder_layer(x, p, tables):
    B, S, _ = x.shape
    x2d = x.reshape(B * S, D_MODEL)
    qkv = _inproj(x2d, p["g1"], p["w_in"], p["gn"], p["gm"])
    oda = _diff_attention(qkv, p["scal"], p["subln"], p["dbias"], p["pos"], B, S)
    ona = _neighborhood_attention(qkv, tables, B, S)
    x1, xn, st = _mix(x2d, oda, ona, p["w_out"], p["g2"], p["wqt"], p["sk"])
    g3 = _route(st)
    y = _peer(x1, xn, p["ut"], p["v"], g3)
    return y.reshape(B, S, D_MODEL)


def kernel(x_prompt, x_sample, ln1_g, w_in, da_q_norm_g, da_k_norm_g, da_lambda_q1, da_lambda_k1,
           da_lambda_q2, da_lambda_k2, da_subln_g, na_q_norm_g, na_k_norm_g, na_rpb, w_out, ln2_g,
           peer_w_query, peer_sub_keys, peer_u, peer_v):
    hp, hs = x_prompt, x_sample
    for l in range(ln1_g.shape[0]):
        p = _prepare_params(l, ln1_g[l], w_in[l], da_q_norm_g[l], da_k_norm_g[l], da_lambda_q1[l],
                            da_lambda_k1[l], da_lambda_q2[l], da_lambda_k2[l], da_subln_g[l],
                            na_q_norm_g[l], na_k_norm_g[l], na_rpb[l], w_out[l], ln2_g[l],
                            peer_w_query[l], peer_sub_keys[l], peer_u[l], peer_v[l])
        outs = []
        for x in (hp, hs):
            tables = _na_bias_tables(p["rpb"], x.shape[1] // GRID_W)
            outs.append(_encoder_layer(x, p, tables))
        hp, hs = outs
    return (hp, hs)
```

```python
import functools
import math

import numpy as np
import jax
import jax.numpy as jnp
from jax import lax
from jax.experimental import pallas as pl
from jax.experimental.pallas import tpu as pltpu

F32 = jnp.float32
BF16 = jnp.bfloat16

D_MODEL = 1024
HEAD_DIM = 64
LANES = 128
MXU_N = 256
DA_HEADS = 4
DA_WIDTH = 512
NA_HEADS = 8
NA_WIDTH = 512
GRID_W = 64
NA_WIN_R = 8
NA_WIN_C = 16
IN_COLS = 3 * DA_WIDTH + 3 * NA_WIDTH
PEER_HEADS = 8
PEER_KEYS = 128
PEER_QDIM = 256
PEER_TOPK = 16
PEER_EXPERTS = PEER_KEYS * PEER_KEYS
RMS_EPS = 1e-6
NEG = -1e30
VMEM_LIMIT = 56 * 1024 * 1024

_DQ, _DK, _DV = 0, 4, 8
_NQ, _NK, _NV = 12, 16, 20


def _cparams(sem):
    return pltpu.CompilerParams(dimension_semantics=sem, vmem_limit_bytes=VMEM_LIMIT)


def _group_ms(x, gm):
    sq = x * x
    hi = sq.astype(BF16)
    lo = (sq - hi.astype(F32)).astype(BF16)
    return (jnp.dot(hi, gm, preferred_element_type=F32)
            + jnp.dot(lo, gm, preferred_element_type=F32))


def _inproj_kernel(x_ref, g1_ref, w_ref, gn_ref, gm_ref, o_ref):
    x = x_ref[...]
    ms = jnp.mean(x * x, axis=-1, keepdims=True)
    xn = (x * lax.rsqrt(ms + RMS_EPS) * g1_ref[...]).astype(BF16)
    proj = jnp.dot(xn, w_ref[...], preferred_element_type=F32)
    gm = gm_ref[...]
    for sec, row in ((0, 0), (1, 1), (3, 2), (4, 3)):
        xs = proj[:, sec * 512:(sec + 1) * 512]
        y = xs * lax.rsqrt(_group_ms(xs, gm) + RMS_EPS) * gn_ref[row:row + 1, :]
        o_ref[:, sec * 512:(sec + 1) * 512] = y.astype(BF16)
    for sec in (2, 5):
        o_ref[:, sec * 512:(sec + 1) * 512] = proj[:, sec * 512:(sec + 1) * 512].astype(BF16)


def _inproj(x2d, g1, w_in_bf, gn, gm, tm=1024):
    T = x2d.shape[0]
    return pl.pallas_call(
        _inproj_kernel,
        grid=(T // tm,),
        in_specs=[
            pl.BlockSpec((tm, D_MODEL), lambda i: (i, 0)),
            pl.BlockSpec((1, D_MODEL), lambda i: (0, 0)),
            pl.BlockSpec((D_MODEL, IN_COLS), lambda i: (0, 0), pipeline_mode=pl.Buffered(1)),
            pl.BlockSpec((4, 512), lambda i: (0, 0)),
            pl.BlockSpec((512, 512), lambda i: (0, 0)),
        ],
        out_specs=pl.BlockSpec((tm, IN_COLS), lambda i: (i, 0)),
        out_shape=jax.ShapeDtypeStruct((T, IN_COLS), BF16),
        compiler_params=_cparams(("parallel",)),
        name="inproj",
    )(x2d, g1, w_in_bf, gn, gm)


DA_TQ = 512
DA_TK = 512
DA_CHUNK = MXU_N
DA_POS_LANES = 6
DA_VROWS = 144
LOG2E = math.log2(math.e)


def _da_kernel(sc_ref, q_ref, qn_ref, k_ref, v_ref, g_ref, db_ref, pos_ref, o_ref,
               qst_ref, qstn_ref, vt_ref, m_ref, acc_ref, sta_ref, stb_ref, *, nk):
    tq, tk = DA_TQ, DA_TK
    h = pl.program_id(1)
    qi = pl.program_id(2)
    nq = pl.num_programs(2)
    slope = sc_ref[h]
    lam = sc_ref[4]
    post = sc_ref[5]
    unroll = nk if nk <= 16 else (8 if nk % 8 == 0 else 2)
    handoff = unroll == nk

    @pl.when(qi == 0)
    def _transpose_values():
        ones_row = jnp.where(lax.broadcasted_iota(jnp.int32, (DA_VROWS - LANES, tk), 0) == 0, 1.0, 0.0)
        for c in range(nk):
            vt_ref[c, 0:LANES, :] = v_ref[c * tk:(c + 1) * tk, :].astype(F32).T.astype(BF16)
            vt_ref[c, LANES:DA_VROWS, :] = ones_row.astype(BF16)

    def query_operand(dst_ref, src_ref):
        qT = src_ref[...].astype(F32).T
        row = lax.broadcasted_iota(jnp.int32, (LANES, tq), 0)
        dst_ref[0:LANES, 0:tq] = jnp.where(row < HEAD_DIM, qT, 0.0).astype(BF16)
        dst_ref[0:LANES, tq:2 * tq] = jnp.where(row >= HEAD_DIM, qT, 0.0).astype(BF16)
        row2 = lax.broadcasted_iota(jnp.int32, (LANES, 2 * tq), 0)
        dst_ref[LANES:2 * LANES, :] = jnp.where(row2 < DA_POS_LANES, 1.0, 0.0).astype(BF16)

    query_operand(qst_ref, q_ref)
    if handoff:
        query_operand(qstn_ref, qn_ref)
    m_ref[...] = jnp.full((1, 2 * tq), NEG, F32)
    acc_ref[...] = jnp.zeros((DA_VROWS, 2 * tq), F32)

    pos = pos_ref[0].astype(F32)
    col = lax.broadcasted_iota(jnp.int32, (1, 2 * tq), 1)
    icol = jnp.where(col >= tq, col - tq, col).astype(F32)

    nch = 2 * tq // DA_CHUNK

    def cols(n):
        return slice(n * DA_CHUNK, (n + 1) * DA_CHUNK)

    def side(blk):
        return jnp.where(qi > blk, 1.0, jnp.where(qi < blk, -1.0, 0.0))

    def key_operand(blk, sgn):
        k0 = pl.multiple_of(blk * tk, tk)
        return jnp.concatenate([k_ref[pl.ds(k0, tk), :], (pos * sgn).astype(BF16)], axis=1)

    def softmax_pv(n, src_ref, vtb, cq):
        sl = cols(n)
        st = src_ref[:, sl]
        c_n = cq[:, sl]
        m_old = m_ref[:, sl]
        m_new = jnp.maximum(m_old, jnp.max(st, axis=0, keepdims=True) + c_n)
        p = jnp.exp2(st - (m_new - c_n)).astype(BF16)
        alpha = jnp.exp2(m_old - m_new)
        acc_ref[:, sl] = alpha * acc_ref[:, sl] + jnp.dot(vtb, p, preferred_element_type=F32)
        m_ref[:, sl] = m_new

    def diagonal_scores(n, ka, qop_ref):
        return jnp.dot(ka, qop_ref[:, cols(n)], preferred_element_type=F32) + db_ref[0, :, cols(n)]

    def step(cur, nxt, src_ref, dst_ref, next_query=False):
        ka = key_operand(nxt, 0.0 if next_query else side(nxt))
        vtb = vt_ref[cur]
        cq = (-side(cur) * slope * LOG2E) * ((qi * tq - cur * tk).astype(F32) + icol)

        def scores(n):
            if next_query:
                dst_ref[:, cols(n)] = diagonal_scores(n, ka, qstn_ref)
            else:
                dst_ref[:, cols(n)] = jnp.dot(ka, qst_ref[:, cols(n)], preferred_element_type=F32)

        scores(0)
        scores(1)
        for n in range(nch):
            softmax_pv(n, src_ref, vtb, cq)
            if n + 2 < nch:
                scores(n + 2)

    def other(x):
        return x + (x >= qi).astype(jnp.int32)

    def first_scores():
        ka0 = key_operand(qi, 0.0)
        for n in range(nch):
            sta_ref[:, cols(n)] = diagonal_scores(n, ka0, qst_ref)

    if handoff:
        pl.when(qi == 0)(first_scores)
        cur = qi
        for s in range(nk):
            src_ref, dst_ref = (sta_ref, stb_ref) if s % 2 == 0 else (stb_ref, sta_ref)
            if s < nk - 1:
                nxt = other(s)
                step(cur, nxt, src_ref, dst_ref)
                cur = nxt
            else:
                step(cur, jnp.minimum(qi + 1, nq - 1), src_ref, dst_ref, next_query=True)
    else:
        first_scores()

        def trip(ii, carry):
            cur = jnp.where(ii == 0, qi, other(unroll * ii - 1))
            for s in range(unroll):
                nxt = other(jnp.minimum(unroll * ii + s, nk - 2))
                src_ref, dst_ref = (sta_ref, stb_ref) if s % 2 == 0 else (stb_ref, sta_ref)
                step(cur, nxt, src_ref, dst_ref)
                cur = nxt
            return carry

        lax.fori_loop(0, nk // unroll, trip, 0)

    on = acc_ref[0:LANES, :] * (1.0 / acc_ref[LANES:LANES + 1, :])
    oT = on[:, 0:tq] - lam * on[:, tq:2 * tq]
    ms = jnp.mean(oT * oT, axis=0, keepdims=True)
    oT = oT * lax.rsqrt(ms + RMS_EPS) * g_ref[...] * post
    o_ref[...] = oT.T.astype(BF16)


def _da_tables(slopes):
    jj = np.arange(DA_TK)[:, None]
    ii = np.concatenate([np.arange(DA_TQ), np.arange(DA_TQ)])[None, :]
    dist = jnp.asarray(np.abs(ii - jj).astype(np.float32))
    dbias = -(slopes * LOG2E)[:, None, None] * dist[None]
    j = np.arange(DA_TK)
    terms = []
    for part in ((j >> 5) * 32.0, (j & 31) * 1.0):
        rem = (slopes * LOG2E)[:, None] * jnp.asarray(part.astype(np.float32))[None, :]
        for _ in range(3):
            piece = rem.astype(BF16)
            terms.append(piece)
            rem = rem - piece.astype(F32)
    pos = jnp.stack(terms, axis=-1)
    pos = jnp.concatenate([pos, jnp.zeros(pos.shape[:2] + (LANES - DA_POS_LANES,), BF16)], axis=-1)
    return dbias, pos


def _diff_attention(qkv, scal, subln_col, dbias, pos, B, S):
    T = B * S
    tq, tk = DA_TQ, DA_TK
    nq, nk = S // tq, S // tk
    kern = functools.partial(_da_kernel, nk=nk)
    return pl.pallas_call(
        kern,
        grid=(B, DA_HEADS, nq),
        in_specs=[
            pl.BlockSpec(memory_space=pltpu.SMEM),
            pl.BlockSpec((tq, LANES), lambda b, h, i: (b * nq + i, _DQ + h)),
            pl.BlockSpec((tq, LANES), lambda b, h, i: (b * nq + jnp.minimum(i + 1, nq - 1), _DQ + h)),
            pl.BlockSpec((S, LANES), lambda b, h, i: (b, _DK + h)),
            pl.BlockSpec((S, LANES), lambda b, h, i: (b, _DV + h)),
            pl.BlockSpec((LANES, 1), lambda b, h, i: (0, 0)),
            pl.BlockSpec((1, tk, 2 * tq), lambda b, h, i: (h, 0, 0)),
            pl.BlockSpec((1, tk, LANES), lambda b, h, i: (h, 0, 0)),
        ],
        out_specs=pl.BlockSpec((tq, LANES), lambda b, h, i: (b * nq + i, h)),
        out_shape=jax.ShapeDtypeStruct((T, DA_WIDTH), BF16),
        scratch_shapes=[
            pltpu.VMEM((2 * LANES, 2 * tq), BF16),
            pltpu.VMEM((2 * LANES, 2 * tq), BF16),
            pltpu.VMEM((nk, DA_VROWS, tk), BF16),
            pltpu.VMEM((1, 2 * tq), F32),
            pltpu.VMEM((DA_VROWS, 2 * tq), F32),
            pltpu.VMEM((tk, 2 * tq), F32),
            pltpu.VMEM((tk, 2 * tq), F32),
        ],
        compiler_params=_cparams(("parallel", "parallel", "arbitrary")),
        name="diff_attn",
    )(scal, qkv, qkv, qkv, qkv, subln_col, dbias, pos)


NA_QROWS = 8
NA_KROWS = 16
NA_TQ = NA_QROWS * GRID_W
NA_TK = NA_KROWS * GRID_W


def _na_bias_tables(rpb, rows):
    cols = np.arange(GRID_W)
    cstart = np.clip(cols - NA_WIN_C // 2, 0, GRID_W - NA_WIN_C)
    kc = cols[None, :]
    col_ok = (kc >= cstart[:, None]) & (kc < cstart[:, None] + NA_WIN_C)
    dc = np.clip(kc - cols[:, None] + (NA_WIN_C - 1), 0, 2 * NA_WIN_C - 2)
    t = jnp.where(jnp.asarray(col_ok)[None, None], rpb[:, :, dc] * LOG2E, NEG)
    t = jnp.concatenate([t, jnp.full((NA_HEADS, 1, GRID_W, GRID_W), NEG, F32)], axis=1)
    tables = []
    for r0, w0 in ((0, 0), (rows // 2 // NA_QROWS * NA_QROWS, None), (rows - NA_QROWS, rows - NA_KROWS)):
        if w0 is None:
            r0 = max(NA_QROWS, min(r0, rows - 2 * NA_QROWS))
            w0 = r0 - NA_WIN_R // 2
        r = r0 + np.arange(NA_QROWS)[:, None]
        kr = w0 + np.arange(NA_KROWS)[None, :]
        rs = np.clip(r - NA_WIN_R // 2, 0, rows - NA_WIN_R)
        ok = (kr >= rs) & (kr < rs + NA_WIN_R)
        dr = np.where(ok, kr - r + (NA_WIN_R - 1), 2 * NA_WIN_R - 1)
        tb = t[:, dr]
        tables.append(jnp.transpose(tb, (0, 1, 3, 2, 4)).reshape(NA_HEADS, NA_TQ, NA_TK))
    return jnp.stack(tables)


def _na_kernel(q_ref, k_ref, v_ref, b_ref, o_ref, *, rows):
    j = pl.program_id(2)
    w0 = jnp.clip(j * NA_QROWS - NA_WIN_R // 2, 0, rows - NA_KROWS)
    start = pl.multiple_of(w0 * GRID_W, GRID_W)
    kw = k_ref[pl.ds(start, NA_TK), :]
    vw = v_ref[pl.ds(start, NA_TK), :]
    half = NA_TQ // 2
    lane = lax.broadcasted_iota(jnp.int32, (half, LANES), 1)
    chains = [(qh, hh) for qh in range(2) for hh in range(2)]

    def scores(qh, hh):
        q = q_ref[qh * half:(qh + 1) * half, :]
        msk = (lane < HEAD_DIM) if hh == 0 else (lane >= HEAD_DIM)
        qm = jnp.where(msk, q, jnp.zeros_like(q))
        s = lax.dot_general(qm, kw, (((1,), (1,)), ((), ())), preferred_element_type=F32)
        return s + b_ref[0, hh, qh * half:(qh + 1) * half, :]

    def attend(s):
        m = jnp.max(s, axis=-1, keepdims=True)
        p = jnp.exp2(s - m)
        l = jnp.sum(p, axis=-1, keepdims=True)
        return jnp.dot(p.astype(BF16), vw, preferred_element_type=F32) / l

    outs = {}
    pending = scores(*chains[0])
    for c, chain in enumerate(chains):
        s = pending
        if c + 1 < len(chains):
            pending = scores(*chains[c + 1])
        outs[chain] = attend(s)
    for qh in range(2):
        o_ref[qh * half:(qh + 1) * half, :] = jnp.where(
            lane < HEAD_DIM, outs[(qh, 0)], outs[(qh, 1)]).astype(BF16)


def _neighborhood_attention(qkv, tables, B, S):
    T = B * S
    rows = S // GRID_W
    nblk = rows // NA_QROWS
    kern = functools.partial(_na_kernel, rows=rows)

    def btype(j):
        return jnp.where(j == 0, 0, jnp.where(j == nblk - 1, 2, 1))

    return pl.pallas_call(
        kern,
        grid=(NA_HEADS // 2, B, nblk),
        in_specs=[
            pl.BlockSpec((NA_TQ, LANES), lambda p, b, j: (b * nblk + j, _NQ + p)),
            pl.BlockSpec((S, LANES), lambda p, b, j: (b, _NK + p)),
            pl.BlockSpec((S, LANES), lambda p, b, j: (b, _NV + p)),
            pl.BlockSpec((1, 2, NA_TQ, NA_TK), lambda p, b, j: (btype(j), p, 0, 0)),
        ],
        out_specs=pl.BlockSpec((NA_TQ, LANES), lambda p, b, j: (b * nblk + j, p)),
        out_shape=jax.ShapeDtypeStruct((T, NA_WIDTH), BF16),
        compiler_params=_cparams(("parallel", "parallel", "parallel")),
        name="nbr_attn",
    )(qkv, qkv, qkv, tables)


def _mix_kernel(x_ref, oda_ref, ona_ref, wo_ref, g2_ref, wqt_ref, sk_ref, x1_ref, xn_ref, st_ref):
    x1 = (x_ref[...]
          + jnp.dot(oda_ref[...], wo_ref[0:DA_WIDTH, :], preferred_element_type=F32)
          + jnp.dot(ona_ref[...], wo_ref[DA_WIDTH:, :], preferred_element_type=F32))
    x1_ref[...] = x1
    ms = jnp.mean(x1 * x1, axis=-1, keepdims=True)
    xn = (x1 * lax.rsqrt(ms + RMS_EPS) * g2_ref[...]).astype(BF16)
    xn_ref[...] = xn
    qt = lax.dot_general(wqt_ref[...], xn, (((1,), (1,)), ((), ())), preferred_element_type=F32).astype(BF16)
    for hp in range(2 * PEER_HEADS):
        st_ref[hp * PEER_KEYS:(hp + 1) * PEER_KEYS, :] = jnp.dot(
            sk_ref[hp], qt[hp * LANES:(hp + 1) * LANES, :], preferred_element_type=F32)


def _mix(x2d, oda, ona, wo_bf, g2, wqt_bf, sk_bf, tm=512):
    T = x2d.shape[0]
    nsc = 2 * PEER_HEADS * PEER_KEYS
    return pl.pallas_call(
        _mix_kernel,
        grid=(T // tm,),
        in_specs=[
            pl.BlockSpec((tm, D_MODEL), lambda i: (i, 0)),
            pl.BlockSpec((tm, DA_WIDTH), lambda i: (i, 0)),
            pl.BlockSpec((tm, NA_WIDTH), lambda i: (i, 0)),
            pl.BlockSpec((D_MODEL, D_MODEL), lambda i: (0, 0), pipeline_mode=pl.Buffered(1)),
            pl.BlockSpec((1, D_MODEL), lambda i: (0, 0)),
            pl.BlockSpec((nsc, D_MODEL), lambda i: (0, 0), pipeline_mode=pl.Buffered(1)),
            pl.BlockSpec((2 * PEER_HEADS, PEER_KEYS, LANES), lambda i: (0, 0, 0)),
        ],
        out_specs=[
            pl.BlockSpec((tm, D_MODEL), lambda i: (i, 0)),
            pl.BlockSpec((tm, D_MODEL), lambda i: (i, 0)),
            pl.BlockSpec((nsc, tm), lambda i: (0, i)),
        ],
        out_shape=[
            jax.ShapeDtypeStruct((T, D_MODEL), F32),
            jax.ShapeDtypeStruct((T, D_MODEL), BF16),
            jax.ShapeDtypeStruct((nsc, T), F32),
        ],
        compiler_params=_cparams(("parallel",)),
        name="mix",
    )(x2d, oda, ona, wo_bf, g2, wqt_bf, sk_bf)


def _sort16_network():
    n, pairs, p = 16, [], 1
    while p < n:
        k = p
        while k >= 1:
            j = k % p
            while j <= n - 1 - k:
                for i in range(min(k - 1, n - j - k - 1) + 1):
                    if (i + j) // (2 * p) == (i + j + k) // (2 * p):
                        pairs.append((i + j, i + j + k))
                j += 2 * k
            k //= 2
        p *= 2
    return pairs


def _top16_tiles(val, idx):
    nt = len(val)
    L = val[0].shape[1]
    val, idx = list(val), list(idx)
    for a, b in [(a, b) for a, b in _sort16_network() if b < nt]:
        swap = (val[b] > val[a]) | ((val[b] == val[a]) & (idx[b] < idx[a]))
        val[a], val[b] = jnp.maximum(val[a], val[b]), jnp.minimum(val[a], val[b])
        idx[a], idx[b] = jnp.where(swap, idx[b], idx[a]), jnp.where(swap, idx[a], idx[b])
    slot = lax.broadcasted_iota(jnp.int32, (PEER_TOPK, L), 0)
    vals = jnp.zeros((PEER_TOPK, L), F32)
    idxs = jnp.zeros((PEER_TOPK, L), F32)
    for it in range(PEER_TOPK):
        m = jnp.max(val[0], axis=0, keepdims=True)
        ix = jnp.min(jnp.where(val[0] == m, idx[0], 1e9), axis=0, keepdims=True)
        vals = jnp.where(slot == it, m, vals)
        idxs = jnp.where(slot == it, ix, idxs)
        win = idx[0] == ix
        live = PEER_TOPK - 1 - it
        for k in range(min(nt - 1, live)):
            val[k] = jnp.where(win, val[k + 1], val[k])
            idx[k] = jnp.where(win, idx[k + 1], idx[k])
        if nt - 1 < live:
            val[nt - 1] = jnp.where(win, -jnp.inf, val[nt - 1])
    return vals, idxs


def _top16_rows(s):
    L = s.shape[1]
    sub = lax.broadcasted_iota(jnp.int32, (8, L), 0).astype(F32)
    ntile = s.shape[0] // 8
    return _top16_tiles([s[8 * v:8 * v + 8, :] for v in range(ntile)],
                        [sub + float(8 * v) for v in range(ntile)])


def _route_topk(st_ref, per_head):
    L = st_ref.shape[1]
    ncand = 72
    r = lax.broadcasted_iota(jnp.int32, (ncand, L), 0)
    dup = ((r >= 48) & (r < 53)) | ((r >= 64) & (r < 69))
    pos = jnp.where(r < 16, r,
                    jnp.where(r < 48, (1 + ((r - 16) >> 3)) * PEER_TOPK + ((r - 16) & 7),
                              jnp.where(r < 64, (r - 48) * PEER_TOPK, (r - 64) * PEER_TOPK + 1)))
    pos = jnp.where(dup, 1000 + r, pos).astype(F32)
    pos_tiles = [pos[8 * v:8 * v + 8, :] for v in range(ncand // 8)]
    slot = lax.broadcasted_iota(jnp.int32, (PEER_TOPK, L), 0)
    slotf = slot.astype(F32)
    his, los, gates = [], [], []
    for h in range(PEER_HEADS):
        s1, i1 = _top16_rows(st_ref[(2 * h) * PEER_KEYS:(2 * h + 1) * PEER_KEYS, :])
        s2, i2 = _top16_rows(st_ref[(2 * h + 1) * PEER_KEYS:(2 * h + 2) * PEER_KEYS, :])
        cand = [s1[0:1, :] + s2[0:8, :], s1[0:1, :] + s2[8:16, :]]
        for a in range(1, 5):
            cand.append(s1[a:a + 1, :] + s2[0:8, :])
        cand += [s1[0:8, :] + s2[0:1, :], s1[8:16, :] + s2[0:1, :], s1[0:8, :] + s2[1:2, :]]
        cand = [jnp.where(dup[8 * v:8 * v + 8, :], -jnp.inf, c) if v in (6, 8) else c
                for v, c in enumerate(cand)]
        top, tpos = _top16_tiles(cand, pos_tiles)
        ta = jnp.floor(tpos * (1.0 / PEER_TOPK))
        tb = tpos - ta * PEER_TOPK
        thi = jnp.zeros((PEER_TOPK, L), F32)
        tlo = jnp.zeros((PEER_TOPK, L), F32)
        for a in range(PEER_TOPK):
            thi = jnp.where(ta == float(a), i1[a:a + 1, :], thi)
            tlo = jnp.where(tb == float(a), i2[a:a + 1, :], tlo)
        e = jnp.exp(top - top[0:1, :])
        his.append(thi)
        los.append(tlo)
        gates.append(e / jnp.sum(e, axis=0, keepdims=True))
        per_head(h)
    return (jnp.concatenate(his, axis=0).T, jnp.concatenate(los, axis=0).T,
            jnp.concatenate(gates, axis=0).T)


GATE_ROWS = 8
ROUTE_L = 128


def _route_gates(hi_ref, lo_ref, g_ref, o_ref, t_lo, t_hi):
    nsel = hi_ref.shape[1]
    rid_b = lax.broadcasted_iota(jnp.int32, (PEER_KEYS, nsel), 0).astype(F32).astype(BF16)
    one_b = jnp.ones((PEER_KEYS, nsel), BF16)
    zero_b = jnp.zeros((PEER_KEYS, nsel), BF16)
    for t in range(t_lo, t_hi):
        hi_b = jnp.broadcast_to(hi_ref[t:t + 1, :].astype(BF16), (PEER_KEYS, nsel))
        lo_b = jnp.broadcast_to(lo_ref[t:t + 1, :].astype(BF16), (PEER_KEYS, nsel))
        g_b = jnp.broadcast_to(g_ref[t:t + 1, :].astype(BF16), (PEER_KEYS, nsel))
        qt = jnp.where(rid_b == hi_b, one_b, zero_b)
        pt = jnp.where(rid_b == lo_b, g_b, zero_b)
        gt = lax.dot_general(qt, pt, (((1,), (1,)), ((), ())), preferred_element_type=F32)
        for jb in range(PEER_EXPERTS // PEER_EBLK):
            o_ref[jb, t] = gt[jb * GATE_ROWS:(jb + 1) * GATE_ROWS, :]


def _route_kernel(st_ref, o_ref, hi_s, lo_s, g_s):
    @pl.when(pl.program_id(0) == 0)
    def _no_previous_block():
        hi_s[...] = jnp.zeros_like(hi_s)
        lo_s[...] = jnp.zeros_like(lo_s)
        g_s[...] = jnp.zeros_like(g_s)

    per = hi_s.shape[0] // PEER_HEADS

    def gates_slice(h):
        _route_gates(hi_s, lo_s, g_s, o_ref, h * per, (h + 1) * per)

    hi, lo, g = _route_topk(st_ref, gates_slice)
    hi_s[...] = hi
    lo_s[...] = lo
    g_s[...] = g


def _route(st):
    nsc, T = st.shape
    L = ROUTE_L
    nblk = T // L
    nsel = PEER_HEADS * PEER_TOPK
    njb = PEER_EXPERTS // PEER_EBLK
    return pl.pallas_call(
        _route_kernel,
        grid=(nblk + 1,),
        in_specs=[pl.BlockSpec((nsc, L), lambda i: (0, jnp.minimum(i, nblk - 1)))],
        out_specs=pl.BlockSpec((njb, L, GATE_ROWS, PEER_KEYS), lambda i: (0, jnp.maximum(i - 1, 0), 0, 0)),
        out_shape=jax.ShapeDtypeStruct((njb, T, GATE_ROWS, PEER_KEYS), F32),
        scratch_shapes=[pltpu.VMEM((L, nsel), F32)] * 3,
        compiler_params=_cparams(("arbitrary",)),
        name="peer_route",
    )(st)


PEER_TM = 1024
PEER_EBLK = 1024
assert GATE_ROWS * PEER_KEYS == PEER_EBLK


def _gelu_tanh(x):
    return 0.5 * x * (1.0 + jnp.tanh(math.sqrt(2.0 / math.pi) * (x + 0.044715 * (x * x * x))))


def _peer_kernel(x1_ref, xn_ref, ut_ref, v_ref, g3_ref, y_ref):
    j = pl.program_id(1)

    @pl.when(j == 0)
    def _residual():
        y_ref[...] = x1_ref[...]

    hmat = jnp.dot(xn_ref[...], ut_ref[...], preferred_element_type=F32)
    tm = xn_ref.shape[0]
    gates = jnp.concatenate([g3_ref[pl.ds(c, tm, stride=GATE_ROWS), :] for c in range(GATE_ROWS)], axis=1)
    w = (gates * _gelu_tanh(hmat)).astype(BF16)
    y_ref[...] += jnp.dot(w, v_ref[...], preferred_element_type=F32)


def _peer(x1, xn, ut_bf, v_bf, g3):
    T = x1.shape[0]
    tm = PEER_TM
    return pl.pallas_call(
        _peer_kernel,
        grid=(T // tm, PEER_EXPERTS // PEER_EBLK),
        in_specs=[
            pl.BlockSpec((tm, D_MODEL), lambda i, j: (i, 0)),
            pl.BlockSpec((tm, D_MODEL), lambda i, j: (i, 0)),
            pl.BlockSpec((D_MODEL, PEER_EBLK), lambda i, j: (0, j)),
            pl.BlockSpec((PEER_EBLK, D_MODEL), lambda i, j: (j, 0)),
            pl.BlockSpec((tm * GATE_ROWS, PEER_KEYS), lambda i, j: (j * (T // tm) + i, 0)),
        ],
        out_specs=pl.BlockSpec((tm, D_MODEL), lambda i, j: (i, 0)),
        out_shape=jax.ShapeDtypeStruct((T, D_MODEL), F32),
        compiler_params=_cparams(("parallel", "arbitrary")),
        name="peer_mlp",
    )(x1, xn, ut_bf, v_bf, g3.reshape(-1, PEER_KEYS))


def _prepare_params(layer_idx, ln1_g, w_in, da_q_norm_g, da_k_norm_g, lam_q1, lam_k1, lam_q2, lam_k2,
                    da_subln_g, na_q_norm_g, na_k_norm_g, na_rpb, w_out, ln2_g,
                    peer_w_query, peer_sub_keys, peer_u, peer_v):
    scale = HEAD_DIM ** -0.5
    gn = jnp.stack([jnp.tile(da_q_norm_g, 8) * (scale * LOG2E), jnp.tile(da_k_norm_g, 8),
                    jnp.tile(na_q_norm_g, 8) * (scale * LOG2E), jnp.tile(na_k_norm_g, 8)]).astype(F32)
    grp = np.arange(512) // HEAD_DIM
    gm = jnp.asarray((grp[:, None] == grp[None, :]).astype(np.float32) / HEAD_DIM, BF16)
    lambda_init = 0.8 - 0.6 * math.exp(-0.3 * layer_idx)
    lam = (jnp.exp(jnp.sum(lam_q1.astype(F32) * lam_k1.astype(F32)))
           - jnp.exp(jnp.sum(lam_q2.astype(F32) * lam_k2.astype(F32))) + lambda_init)
    slopes = jnp.asarray([2.0 ** (-8.0 * (h + 1) / DA_HEADS) for h in range(DA_HEADS)], F32)
    scal = jnp.concatenate([slopes, lam.reshape(1), jnp.full((1,), 1.0 - lambda_init, F32)])
    dbias, pos = _da_tables(slopes)
    return dict(
        g1=ln1_g.reshape(1, D_MODEL), w_in=w_in.astype(BF16), gn=gn, gm=gm, scal=scal,
        subln=da_subln_g.reshape(LANES, 1), dbias=dbias, pos=pos, rpb=na_rpb,
        w_out=w_out.astype(BF16), g2=ln2_g.reshape(1, D_MODEL), wqt=peer_w_query.T.astype(BF16),
        sk=peer_sub_keys.reshape(2 * PEER_HEADS, PEER_KEYS, LANES).astype(BF16),
        ut=peer_u.T.astype(BF16), v=peer_v.astype(BF16))


def _encoder_layer(x, p, tables):
    B, S, _ = x.shape
    x2d = x.reshape(B * S, D_MODEL)
    qkv = _inproj(x2d, p["g1"], p["w_in"], p["gn"], p["gm"])
    oda = _diff_attention(qkv, p["scal"], p["subln"], p["dbias"], p["pos"], B, S)
    ona = _neighborhood_attention(qkv, tables, B, S)
    x1, xn, st = _mix(x2d, oda, ona, p["w_out"], p["g2"], p["wqt"], p["sk"])
    g3 = _route(st)
    y = _peer(x1, xn, p["ut"], p["v"], g3)
    return y.reshape(B, S, D_MODEL)


def kernel(x_prompt, x_sample, ln1_g, w_in, da_q_norm_g, da_k_norm_g, da_lambda_q1, da_lambda_k1,
           da_lambda_q2, da_lambda_k2, da_subln_g, na_q_norm_g, na_k_norm_g, na_rpb, w_out, ln2_g,
           peer_w_query, peer_sub_keys, peer_u, peer_v):
    hp, hs = x_prompt, x_sample
    for l in range(ln1_g.shape[0]):
        p = _prepare_params(l, ln1_g[l], w_in[l], da_q_norm_g[l], da_k_norm_g[l], da_lambda_q1[l],
                            da_lambda_k1[l], da_lambda_q2[l], da_lambda_k2[l], da_subln_g[l],
                            na_q_norm_g[l], na_k_norm_g[l], na_rpb[l], w_out[l], ln2_g[l],
                            peer_w_query[l], peer_sub_keys[l], peer_u[l], peer_v[l])
        outs = []
        for x in (hp, hs):
            tables = _na_bias_tables(p["rpb"], x.shape[1] // GRID_W)
            outs.append(_encoder_layer(x, p, tables))
        hp, hs = outs
    return (hp, hs)
```
